```python
import jax, jax.numpy as jnp
from jax import lax
import numpy as np

D_MODEL = 1024
BATCH = 16
SEQ = 2048
DEPTH = 2

GRID_W = 64
HEAD_DIM = 64
A_HEADS = 8
A_KV_HEADS = 2
A_GROUPS = A_HEADS // A_KV_HEADS
Q_BLOCK = 128
ROPE_THETA = 10000.0
POOL_WINDOWS = (2, 4, 8, 16)
POOL_GROUP = 128
POOL_WIDTH = POOL_GROUP * len(POOL_WINDOWS)
C_HEADS = 8
NA_ROWS = 8
NA_COLS = 16
A_Q = A_HEADS * HEAD_DIM
A_KV = A_KV_HEADS * HEAD_DIM
C_W = C_HEADS * HEAD_DIM
N_BRANCH = 3
BRANCH_WIDTH = 512
IN_WIDTHS = (A_Q, A_KV, A_KV, POOL_WIDTH, C_W, C_W, C_W, N_BRANCH * D_MODEL)
D_IN = sum(IN_WIDTHS)
D_FF = 2816
CONV_W = 3
EPS = 1e-6

kernel_name = "hybrid_gated_axial_pool_natten_encoder"


def rmsnorm(x, g):
    xf = x.astype(jnp.float32)
    y = xf * lax.rsqrt(jnp.mean(xf * xf, axis=-1, keepdims=True) + EPS)
    return (y * g.astype(jnp.float32)).astype(x.dtype)


def rope_axis(x, pos):
    dim = x.shape[-1]
    half = dim // 2
    freqs = ROPE_THETA ** (-jnp.arange(0, dim, 2, dtype=jnp.float32) / dim)
    ang = pos.astype(jnp.float32)[:, None] * freqs[None, :]
    cos = jnp.cos(ang)[:, None, :]
    sin = jnp.sin(ang)[:, None, :]
    xf = x.astype(jnp.float32)
    x1, x2 = xf[..., :half], xf[..., half:]
    out = jnp.concatenate([x1 * cos - x2 * sin, x1 * sin + x2 * cos], axis=-1)
    return out.astype(x.dtype)


def rope_2d(x, row, col):
    half = x.shape[-1] // 2
    return jnp.concatenate([rope_axis(x[..., :half], row), rope_axis(x[..., half:], col)], axis=-1)


def global_axial_attention(q, k, v, gq, gk):
    B, S = q.shape[0], q.shape[1]
    t = jnp.arange(S)
    row, col = t // GRID_W, t % GRID_W
    q = rmsnorm(q.reshape(B, S, A_HEADS, HEAD_DIM), gq)
    k = rmsnorm(k.reshape(B, S, A_KV_HEADS, HEAD_DIM), gk)
    v = v.reshape(B, S, A_KV_HEADS, HEAD_DIM)
    q = rope_2d(q, row, col) * (HEAD_DIM ** -0.5)
    k = rope_2d(k, row, col)
    nb = S // Q_BLOCK
    qb = q.reshape(B, nb, Q_BLOCK, A_KV_HEADS, A_GROUPS, HEAD_DIM).transpose(1, 0, 2, 3, 4, 5)

    def block(qi):
        s = jnp.einsum('bqkgd,bskd->bkgqs', qi, k).astype(jnp.float32)
        p = jax.nn.softmax(s, axis=-1)
        return jnp.einsum('bkgqs,bskd->bqkgd', p.astype(v.dtype), v)

    o = lax.map(block, qb)
    return o.transpose(1, 0, 2, 3, 4, 5).reshape(B, S, A_Q)


def pool_mixer(u, w_pool, pool_scale):
    B, S = u.shape[0], u.shape[1]
    uf = u.astype(jnp.float32)
    cs = jnp.concatenate([jnp.zeros((B, 1, POOL_WIDTH), jnp.float32), jnp.cumsum(uf, axis=1)], axis=1)
    t = jnp.arange(S)
    outs = []
    for g, w in enumerate(POOL_WINDOWS):
        sl = slice(g * POOL_GROUP, (g + 1) * POOL_GROUP)
        lo = jnp.clip(t - w // 2, 0, S)
        hi = jnp.clip(t + w - w // 2, 0, S)
        csg = cs[..., sl]
        mean = (csg[:, hi] - csg[:, lo]) / (hi - lo).astype(jnp.float32)[None, :, None]
        diff = (mean - uf[..., sl]).astype(u.dtype)
        outs.append(jnp.einsum('bsc,cd->bsd', diff, w_pool[g]))
    return jnp.concatenate(outs, axis=-1) * pool_scale


def neighbourhood_attention(q, k, v, rpb):
    B, S = q.shape[0], q.shape[1]
    rows = S // GRID_W
    win_r = min(NA_ROWS, rows)
    qg = q.reshape(B, rows, GRID_W, C_HEADS, HEAD_DIM) * (HEAD_DIM ** -0.5)
    kg = k.reshape(B, rows, GRID_W, C_HEADS, HEAD_DIM)
    vg = v.reshape(B, rows, GRID_W, C_HEADS, HEAD_DIM)
    col = jnp.arange(GRID_W)
    col_start = jnp.clip(col - NA_COLS // 2, 0, GRID_W - NA_COLS)
    col_idx = col_start[:, None] + jnp.arange(NA_COLS)[None, :]
    dc = col_idx - col[:, None] + (NA_COLS - 1)

    def row_block(args):
        r, q_row = args
        rs = jnp.clip(r - win_r // 2, 0, rows - win_r)
        k_rows = lax.dynamic_slice_in_dim(kg, rs, win_r, axis=1)
        v_rows = lax.dynamic_slice_in_dim(vg, rs, win_r, axis=1)
        k_nb = k_rows[:, :, col_idx]
        v_nb = v_rows[:, :, col_idx]
        dr = rs + jnp.arange(win_r) - r + (NA_ROWS - 1)
        bias = rpb[:, dr[None, :, None], dc[:, None, :]]
        s = jnp.einsum('bqhd,brqjhd->bhqrj', q_row, k_nb).astype(jnp.float32)
        s = s + bias.astype(jnp.float32)[None]
        p = jax.nn.softmax(s.reshape(B, C_HEADS, GRID_W, win_r * NA_COLS), axis=-1)
        p = p.reshape(B, C_HEADS, GRID_W, win_r, NA_COLS).astype(v.dtype)
        return jnp.einsum('bhqrj,brqjhd->bqhd', p, v_nb)

    o = lax.map(row_block, (jnp.arange(rows), qg.transpose(1, 0, 2, 3, 4)))
    return o.transpose(1, 0, 2, 3, 4).reshape(B, S, C_W)


def conv_gated_mlp(h, w_up, conv_w, conv_b, w_down):
    up = jnp.einsum('bsd,df->bsf', h, w_up)
    pad = jnp.pad(up, ((0, 0), (1, 1), (0, 0)))
    up = pad[:, :-2] * conv_w[0] + pad[:, 1:-1] * conv_w[1] + pad[:, 2:] * conv_w[2] + conv_b
    val, gate = up[..., :D_FF], up[..., D_FF:]
    return jnp.einsum('bsf,fd->bsd', jax.nn.gelu(gate, approximate=True) * val, w_down)


def setup_inputs(seed: int = 0) -> dict:
    key = jax.random.key(seed)
    ks = jax.random.split(key, 20)
    n = lambda k, shape, s: jax.random.normal(k, shape, jnp.float32) * s
    L, D = DEPTH, D_MODEL
    return {
        "x": n(ks[0], (BATCH, SEQ, D), 1.0),
        "norm_mix_pre": 1.0 + n(ks[1], (L, D), 0.1),
        "norm_mix_post": 1.0 + n(ks[2], (L, D), 0.1),
        "norm_ffn_pre": 1.0 + n(ks[3], (L, D), 0.1),
        "norm_ffn_post": 1.0 + n(ks[4], (L, D), 0.1),
        "w_in": n(ks[5], (L, D, D_IN), D ** -0.5),
        "b_gate": n(ks[6], (L, N_BRANCH * D), 0.1),
        "qk_norm_q": 1.0 + n(ks[7], (L, HEAD_DIM), 0.1),
        "qk_norm_k": 1.0 + n(ks[8], (L, HEAD_DIM), 0.1),
        "w_pool": n(ks[9], (L, len(POOL_WINDOWS), POOL_GROUP, POOL_GROUP), POOL_GROUP ** -0.5),
        "pool_scale": 1.0 + n(ks[10], (L, POOL_WIDTH), 0.1),
        "rpb": n(ks[11], (L, C_HEADS, 2 * NA_ROWS - 1, 2 * NA_COLS - 1), 0.5),
        "w_branch": n(ks[12], (L, N_BRANCH, BRANCH_WIDTH, D), BRANCH_WIDTH ** -0.5),
        "w_out": n(ks[13], (L, D, D), D ** -0.5),
        "w_up": n(ks[14], (L, D, 2 * D_FF), D ** -0.5),
        "conv_w": n(ks[15], (L, CONV_W, 2 * D_FF), CONV_W ** -0.5),
        "conv_b": n(ks[16], (L, 2 * D_FF), 0.01),
        "w_down": n(ks[17], (L, D_FF, D), D_FF ** -0.5),
    }


def reference(x, norm_mix_pre, norm_mix_post, norm_ffn_pre, norm_ffn_post, w_in, b_gate,
              qk_norm_q, qk_norm_k, w_pool, pool_scale, rpb, w_branch, w_out,
              w_up, conv_w, conv_b, w_down):
    B, S = x.shape[0], x.shape[1]
    splits = [int(s) for s in np.cumsum(IN_WIDTHS)[:-1]]
    for l in range(DEPTH):
        h = rmsnorm(x, norm_mix_pre[l])
        proj = jnp.einsum('bsd,de->bse', h, w_in[l])
        qa, ka, va, pu, qc, kc, vc, gl = jnp.split(proj, splits, axis=-1)
        ya = global_axial_attention(qa, ka, va, qk_norm_q[l], qk_norm_k[l])
        yb = pool_mixer(pu, w_pool[l], pool_scale[l])
        yc = neighbourhood_attention(qc, kc, vc, rpb[l])
        ys = jnp.stack([ya, yb, yc], axis=2)
        z = jnp.einsum('bsnc,ncd->bsnd', ys, w_branch[l])
        gates = jax.nn.sigmoid(gl + b_gate[l]).reshape(B, S, N_BRANCH, D_MODEL)
        merged = jnp.sum(gates * z, axis=2)
        mix = jnp.einsum('bsd,de->bse', merged, w_out[l])
        x = x + rmsnorm(mix, norm_mix_post[l])
        h = rmsnorm(x, norm_ffn_pre[l])
        f = conv_gated_mlp(h, w_up[l], conv_w[l], conv_b[l], w_down[l])
        x = x + rmsnorm(f, norm_ffn_post[l])
    return x
```

```python
import functools

import numpy as np
import jax
import jax.numpy as jnp
from jax import lax
from jax.experimental import pallas as pl
from jax.experimental.pallas import tpu as pltpu

F32 = jnp.float32
BF16 = jnp.bfloat16

D_MODEL = 1024
GRID_W = 64
HEAD_DIM = 64
A_HEADS = 8
A_KV_HEADS = 2
ROPE_THETA = 10000.0
POOL_WINDOWS = (2, 4, 8, 16)
POOL_GROUP = 128
C_HEADS = 8
NA_ROWS = 8
NA_COLS = 16
N_BRANCH = 3
BRANCH_WIDTH = 512
D_FF = 2816
EPS = 1e-6
QK_SCALE = HEAD_DIM ** -0.5

LANES = 128
V7X_VMEM_BYTES = 64 * 1024 * 1024
VMEM_LIMIT = 56 * 1024 * 1024

TOKEN_TILE = 512
ATTN_Q_TILE = 256
POOL_ROW_CHUNK = 256
POOL_HALO = 16
FF_CHUNK = 256
MASK_VALUE = -1e30

OFF_QA = 0
OFF_KVA = 512
OFF_POOL = 768
OFF_QC = 1280
OFF_KC = 1792
OFF_VC = 2304
W_QKV = 2816


def _params(n_grid):
    return pltpu.CompilerParams(
        dimension_semantics=("arbitrary",) * n_grid, vmem_limit_bytes=VMEM_LIMIT)


def _resident(shape):
    zeros = (0,) * len(shape)
    return pl.BlockSpec(shape, lambda *_: zeros, pipeline_mode=pl.Buffered(1))


def _rms(x):
    return x * lax.rsqrt(jnp.mean(x * x, axis=-1, keepdims=True) + EPS)


def _dot(a, b):
    return jnp.dot(a, b, preferred_element_type=F32)


def _dot_nt(a, b):
    return lax.dot_general(a, b, (((1,), (1,)), ((), ())), preferred_element_type=F32)


def _inproj_kernel(x_ref, g_ref, w_ref, cq_ref, sq_ref, ck_ref, sk_ref, ones_ref,
                   qa_ref, ka_ref, va_ref, pu_ref, qc_ref, kc_ref, vc_ref,
                   *, tm, tiles_per_seq):
    i = pl.program_id(0)
    h = (_rms(x_ref[...]) * g_ref[...]).astype(BF16)
    row0 = pl.multiple_of((i % tiles_per_seq) * tm, tm)
    lane = lax.broadcasted_iota(jnp.int32, (tm, LANES), 1)
    first_half = (lane % 32) < 16

    def norm_rope(u, c_ref, s_ref):
        u2 = u * u
        hi = u2.astype(BF16)
        lo = (u2 - hi.astype(F32)).astype(BF16)
        ms = _dot(jnp.concatenate([hi, lo], axis=1), ones_ref[...])
        partner = jnp.where(first_half, pltpu.roll(u, LANES - 16, 1), pltpu.roll(u, 16, 1))
        c = c_ref[pl.ds(row0, tm), :]
        s = s_ref[pl.ds(row0, tm), :]
        return (u * c + partner * s) * lax.rsqrt(ms + EPS)

    qa = _dot(h, w_ref[:, OFF_QA:OFF_QA + 512])
    for c in range(4):
        sl = slice(c * LANES, (c + 1) * LANES)
        qa_ref[:, sl] = norm_rope(qa[:, sl], cq_ref, sq_ref).astype(BF16)
    kv = _dot(h, w_ref[:, OFF_KVA:OFF_KVA + 256])
    ka_ref[...] = norm_rope(kv[:, :LANES], ck_ref, sk_ref).astype(BF16)
    va_ref[...] = kv[:, LANES:].astype(BF16)
    pu_ref[...] = _dot(h, w_ref[:, OFF_POOL:OFF_POOL + 512])
    qc_ref[...] = (_dot(h, w_ref[:, OFF_QC:OFF_QC + 512]) * QK_SCALE).astype(BF16)
    kc_ref[...] = _dot(h, w_ref[:, OFF_KC:OFF_KC + 512]).astype(BF16)
    vc_ref[...] = _dot(h, w_ref[:, OFF_VC:OFF_VC + 512]).astype(BF16)


def _inproj(x2, g_pre, w_qkv, cq, sq, ck, sk, ones_bd, seq):
    t = x2.shape[0]
    tm = TOKEN_TILE
    tiles_per_seq = seq // tm
    row = lambda width: pl.BlockSpec((tm, width), lambda i: (i, 0))
    out_widths = (512, LANES, LANES, 512, 512, 512, 512)
    out_dtypes = (BF16, BF16, BF16, F32, BF16, BF16, BF16)
    return pl.pallas_call(
        functools.partial(_inproj_kernel, tm=tm, tiles_per_seq=tiles_per_seq),
        grid=(t // tm,),
        in_specs=[row(D_MODEL), _resident((1, D_MODEL)), _resident((D_MODEL, W_QKV)),
                  _resident((seq, LANES)), _resident((seq, LANES)),
                  _resident((seq, LANES)), _resident((seq, LANES)),
                  _resident((2 * LANES, LANES))],
        out_specs=[row(w) for w in out_widths],
        out_shape=[jax.ShapeDtypeStruct((t, w), d) for w, d in zip(out_widths, out_dtypes)],
        compiler_params=_params(1),
        name="inproj",
    )(x2, g_pre, w_qkv, cq, sq, ck, sk, ones_bd)


def _split_stack(x, lo):
    zero = jnp.zeros_like(x)
    return jnp.concatenate([jnp.where(lo, x, zero), jnp.where(lo, zero, x)], axis=0)


def _softmax_pv(s, v):
    m = jnp.max(s, axis=-1, keepdims=True)
    p = jnp.exp(s - m)
    l = jnp.sum(p, axis=-1, keepdims=True)
    return _dot(p.astype(BF16), v) / l


def _attn_kernel(q_ref, k_ref, v_ref, o_ref, *, tq):
    k = k_ref[...]
    v = v_ref[...]
    lo = lax.broadcasted_iota(jnp.int32, (tq, LANES), 1) < HEAD_DIM
    for c in range(4):
        sl = slice(c * LANES, (c + 1) * LANES)
        s = _dot_nt(_split_stack(q_ref[:, sl], lo), k)
        pv = _softmax_pv(s, v)
        o_ref[:, sl] = jnp.where(lo, pv[:tq], pv[tq:]).astype(o_ref.dtype)


def _global_attention(qa, ka, va, seq):
    t = qa.shape[0]
    tq = ATTN_Q_TILE
    nq = seq // tq
    return pl.pallas_call(
        functools.partial(_attn_kernel, tq=tq),
        grid=(t // seq, nq),
        in_specs=[pl.BlockSpec((tq, 512), lambda b, j: (b * nq + j, 0)),
                  pl.BlockSpec((seq, LANES), lambda b, j: (b, 0)),
                  pl.BlockSpec((seq, LANES), lambda b, j: (b, 0))],
        out_specs=pl.BlockSpec((tq, 512), lambda b, j: (b * nq + j, 0)),
        out_shape=jax.ShapeDtypeStruct((t, 512), BF16),
        compiler_params=_params(2),
        name="global_attention",
    )(qa, ka, va)


def _pool_kernel(u_ref, w_ref, scale_ref, o_ref, pad_ref, *, seq):
    width = u_ref.shape[1]
    zeros = jnp.zeros((POOL_HALO, width), F32)
    pad_ref[0:POOL_HALO, :] = zeros
    pad_ref[POOL_HALO + seq:, :] = zeros
    pad_ref[POOL_HALO:POOL_HALO + seq, :] = u_ref[...]
    rc = POOL_ROW_CHUNK
    for r0 in range(0, seq, rc):
        t = r0 + lax.broadcasted_iota(jnp.int32, (rc, POOL_GROUP), 0)
        for g, w in enumerate(POOL_WINDOWS):
            sl = slice(g * POOL_GROUP, (g + 1) * POOL_GROUP)
            half = w // 2
            acc = pad_ref[POOL_HALO + r0 - half:POOL_HALO + r0 - half + rc, sl]
            for d in range(-half + 1, w - half):
                acc = acc + pad_ref[POOL_HALO + r0 + d:POOL_HALO + r0 + d + rc, sl]
            cnt = jnp.minimum(t + (w - half), seq) - jnp.maximum(t - half, 0)
            mean = acc / cnt.astype(F32)
            diff = (mean - u_ref[r0:r0 + rc, sl]).astype(BF16)
            y = _dot(diff, w_ref[g]) * scale_ref[:, sl]
            o_ref[r0:r0 + rc, sl] = y.astype(o_ref.dtype)


def _pool_mixer(pu, w_pool, pool_scale, seq):
    t, width = pu.shape
    n_g = len(POOL_WINDOWS)
    return pl.pallas_call(
        functools.partial(_pool_kernel, seq=seq),
        grid=(t // seq,),
        in_specs=[pl.BlockSpec((seq, width), lambda b: (b, 0)),
                  _resident((n_g, POOL_GROUP, POOL_GROUP)),
                  _resident((1, width))],
        out_specs=pl.BlockSpec((seq, width), lambda b: (b, 0)),
        out_shape=jax.ShapeDtypeStruct((t, width), BF16),
        scratch_shapes=[pltpu.VMEM((seq + 2 * POOL_HALO, width), F32)],
        compiler_params=_params(1),
        name="pool_mixer",
    )(pu, w_pool, pool_scale)


def _natten_kernel(q_ref, k_ref, v_ref, bias_ref, o_ref, *, rows):
    win = NA_ROWS * GRID_W
    lo = lax.broadcasted_iota(jnp.int32, (GRID_W, LANES), 1) < HEAD_DIM

    def body(r, carry):
        rs = jnp.clip(r - NA_ROWS // 2, 0, rows - NA_ROWS)
        kind = rs - r + (NA_ROWS - 1)
        q0 = pl.multiple_of(r * GRID_W, GRID_W)
        k0 = pl.multiple_of(rs * GRID_W, GRID_W)
        for pr in range(C_HEADS // 2):
            sl = slice(pr * LANES, (pr + 1) * LANES)
            qs = _split_stack(q_ref[pl.ds(q0, GRID_W), sl], lo)
            s = _dot_nt(qs, k_ref[pl.ds(k0, win), sl]) + bias_ref[kind, pr]
            pv = _softmax_pv(s, v_ref[pl.ds(k0, win), sl])
            o_ref[pl.ds(q0, GRID_W), sl] = jnp.where(lo, pv[:GRID_W], pv[GRID_W:]).astype(o_ref.dtype)
        return carry

    lax.fori_loop(0, rows, body, 0)


def _neighbourhood_attention(qc, kc, vc, bias, seq):
    t, width = qc.shape
    rows = seq // GRID_W
    blk = pl.BlockSpec((seq, width), lambda b: (b, 0))
    return pl.pallas_call(
        functools.partial(_natten_kernel, rows=rows),
        grid=(t // seq,),
        in_specs=[blk, blk, blk, _resident(bias.shape)],
        out_specs=blk,
        out_shape=jax.ShapeDtypeStruct((t, width), BF16),
        compiler_params=_params(1),
        name="neighbourhood_attention",
    )(qc, kc, vc, bias)


def _merge_kernel(x_ref, ya_ref, yb_ref, yc_ref, gpre_ref, wg_ref, bg_ref, wb_ref, wo_ref,
                  gpost_ref, o_ref):
    x = x_ref[...]
    h = (_rms(x) * gpre_ref[...]).astype(BF16)
    merged = None
    for n, y_ref in enumerate((ya_ref, yb_ref, yc_ref)):
        sl = slice(n * D_MODEL, (n + 1) * D_MODEL)
        gate = jax.nn.sigmoid(_dot(h, wg_ref[:, sl]) + bg_ref[:, sl])
        term = gate * _dot(y_ref[...], wb_ref[n])
        merged = term if merged is None else merged + term
    mix = _dot(merged.astype(BF16), wo_ref[...])
    o_ref[...] = x + _rms(mix) * gpost_ref[...]


def _merge(x2, ya, yb, yc, g_pre, w_gate, b_gate, w_branch, w_out, g_post):
    t = x2.shape[0]
    tm = TOKEN_TILE
    row = lambda width: pl.BlockSpec((tm, width), lambda i: (i, 0))
    return pl.pallas_call(
        _merge_kernel,
        grid=(t // tm,),
        in_specs=[row(D_MODEL), row(BRANCH_WIDTH), row(BRANCH_WIDTH), row(BRANCH_WIDTH),
                  _resident((1, D_MODEL)), _resident((D_MODEL, N_BRANCH * D_MODEL)),
                  _resident((1, N_BRANCH * D_MODEL)),
                  _resident((N_BRANCH, BRANCH_WIDTH, D_MODEL)),
                  _resident((D_MODEL, D_MODEL)), _resident((1, D_MODEL))],
        out_specs=row(D_MODEL),
        out_shape=jax.ShapeDtypeStruct((t, D_MODEL), F32),
        compiler_params=_params(1),
        name="merge",
    )(x2, ya, yb, yc, g_pre, w_gate, b_gate, w_branch, w_out, g_post)


def _ffn_kernel(x_ref, xp_ref, xn_ref, gpre_ref, wup_ref, cw_ref, cb_ref, wdn_ref, gpost_ref,
                o_ref, acc_ref, *, tm, tiles_per_seq, n_chunks):
    i = pl.program_id(0)
    pos = i % tiles_per_seq
    has_prev = (pos > 0).astype(F32)
    has_next = (pos < tiles_per_seq - 1).astype(F32)
    xs = jnp.concatenate([x_ref[...], xp_ref[...] * has_prev, xn_ref[...] * has_next], axis=0)
    h = (_rms(xs) * gpre_ref[...]).astype(BF16)
    row = lax.broadcasted_iota(jnp.int32, (tm, FF_CHUNK), 0)
    is_first = row == 0
    is_last = row == tm - 1

    def conv(up, half, j):
        main = up[:tm]
        before = jnp.where(is_first, up[tm + 7:tm + 8], pltpu.roll(main, 1, 0))
        after = jnp.where(is_last, up[tm + 8:tm + 9], pltpu.roll(main, tm - 1, 0))
        return (before * cw_ref[0, half, j] + main * cw_ref[1, half, j]
                + after * cw_ref[2, half, j] + cb_ref[half, j])

    def chunk(j, carry):
        val = conv(_dot(h, wup_ref[0, j]), 0, j)
        gate = conv(_dot(h, wup_ref[1, j]), 1, j)
        act = (jax.nn.gelu(gate, approximate=True) * val).astype(BF16)
        part = _dot(act, wdn_ref[j])

        @pl.when(j == 0)
        def _():
            acc_ref[...] = part

        @pl.when(j > 0)
        def _():
            acc_ref[...] += part

        return carry

    lax.fori_loop(0, n_chunks, chunk, 0)
    o_ref[...] = x_ref[...] + _rms(acc_ref[...]) * gpost_ref[...]


def _ffn(x2, g_pre, w_up, conv_w, conv_b, w_down, g_post, seq):
    t = x2.shape[0]
    tm = TOKEN_TILE
    tiles_per_seq = seq // tm
    n_chunks = D_FF // FF_CHUNK
    sub = tm // 8
    last = t // 8 - 1
    return pl.pallas_call(
        functools.partial(_ffn_kernel, tm=tm, tiles_per_seq=tiles_per_seq, n_chunks=n_chunks),
        grid=(t // tm,),
        in_specs=[pl.BlockSpec((tm, D_MODEL), lambda i: (i, 0)),
                  pl.BlockSpec((8, D_MODEL), lambda i: (jnp.maximum(i * sub - 1, 0), 0)),
                  pl.BlockSpec((8, D_MODEL), lambda i: (jnp.minimum((i + 1) * sub, last), 0)),
                  _resident((1, D_MODEL)),
                  _resident((2, n_chunks, D_MODEL, FF_CHUNK)),
                  _resident((3, 2, n_chunks, 1, FF_CHUNK)),
                  _resident((2, n_chunks, 1, FF_CHUNK)),
                  _resident((n_chunks, FF_CHUNK, D_MODEL)),
                  _resident((1, D_MODEL))],
        out_specs=pl.BlockSpec((tm, D_MODEL), lambda i: (i, 0)),
        out_shape=jax.ShapeDtypeStruct((t, D_MODEL), F32),
        scratch_shapes=[pltpu.VMEM((tm, D_MODEL), F32)],
        compiler_params=_params(1),
        name="conv_gated_mlp",
    )(x2, x2, x2, g_pre, w_up, conv_w, conv_b, w_down, g_post)


def _qa_column_order():
    cols = []
    for c in range(A_HEADS // A_KV_HEADS):
        cols.append(np.arange(c * HEAD_DIM, (c + 1) * HEAD_DIM))
        cols.append(np.arange((4 + c) * HEAD_DIM, (5 + c) * HEAD_DIM))
    return np.concatenate(cols)


def _rope_tables(gain, scale, seq):
    t = jnp.arange(seq)
    half = HEAD_DIM // 2
    freqs = ROPE_THETA ** (-jnp.arange(0, half, 2, dtype=F32) / half)

    def axis_tables(pos):
        ang = pos.astype(F32)[:, None] * freqs[None, :]
        c, s = jnp.cos(ang), jnp.sin(ang)
        return jnp.concatenate([c, c], -1), jnp.concatenate([-s, s], -1)

    cr, sr = axis_tables(t // GRID_W)
    cc, sc = axis_tables(t % GRID_W)
    cos = jnp.concatenate([cr, cc], -1)
    sin = jnp.concatenate([sr, sc], -1)
    j = np.arange(HEAD_DIM)
    partner = np.where((j % half) < half // 2, j + half // 2, j - half // 2)
    g = gain.astype(F32)
    ctab = cos * (g * scale)[None, :]
    stab = sin * (g[partner] * scale)[None, :]
    return jnp.tile(ctab, (1, 2)), jnp.tile(stab, (1, 2))


def _natten_bias(rpb):
    c = np.arange(GRID_W)
    cs = np.clip(c - NA_COLS // 2, 0, GRID_W - NA_COLS)
    kc = np.arange(GRID_W)
    dc = kc[None, :] - c[:, None] + (NA_COLS - 1)
    valid = (kc[None, :] >= cs[:, None]) & (kc[None, :] < cs[:, None] + NA_COLS)
    row_t = jnp.where(valid[None, None], rpb.astype(F32)[:, :, np.clip(dc, 0, 2 * NA_COLS - 2)],
                      MASK_VALUE)
    dr = np.arange(NA_ROWS)[:, None] + np.arange(NA_ROWS)[None, :]
    tab = row_t[:, dr]
    tab = tab.transpose(1, 0, 3, 2, 4)
    return tab.reshape(NA_ROWS, C_HEADS // 2, 2 * GRID_W, NA_ROWS * GRID_W)


def kernel(x, norm_mix_pre, norm_mix_post, norm_ffn_pre, norm_ffn_post, w_in, b_gate,
           qk_norm_q, qk_norm_k, w_pool, pool_scale, rpb, w_branch, w_out,
           w_up, conv_w, conv_b, w_down):
    b, seq, d = x.shape
    depth = w_in.shape[0]
    assert d == D_MODEL and seq % TOKEN_TILE == 0 and seq % ATTN_Q_TILE == 0
    assert seq // GRID_W >= NA_ROWS and seq % POOL_ROW_CHUNK == 0 and D_FF % FF_CHUNK == 0
    n_chunks = D_FF // FF_CHUNK
    order = _qa_column_order()
    hh = np.arange(2 * LANES) % LANES // HEAD_DIM
    ones_bd = jnp.asarray((hh[:, None] == np.arange(LANES)[None, :] // HEAD_DIM) / HEAD_DIM, BF16)
    row2 = lambda v: v.astype(F32).reshape(1, -1)

    x2 = x.reshape(b * seq, d)
    for l in range(depth):
        w_l = w_in[l]
        w_qkv = jnp.concatenate([w_l[:, :512][:, order], w_l[:, 512:W_QKV]], axis=1).astype(BF16)
        w_gate = w_l[:, W_QKV:].astype(BF16)
        cq, sq = _rope_tables(qk_norm_q[l], QK_SCALE, seq)
        ck, sk = _rope_tables(qk_norm_k[l], 1.0, seq)
        wb = w_branch[l]
        wb = jnp.concatenate([wb[0][order][None], wb[1:]], axis=0).astype(BF16)
        wup = w_up[l].reshape(D_MODEL, 2, n_chunks, FF_CHUNK).transpose(1, 2, 0, 3).astype(BF16)
        cw = conv_w[l].astype(F32).reshape(3, 2, n_chunks, 1, FF_CHUNK)
        cb = conv_b[l].astype(F32).reshape(2, n_chunks, 1, FF_CHUNK)
        wdn = w_down[l].reshape(n_chunks, FF_CHUNK, D_MODEL).astype(BF16)

        g_pre = row2(norm_mix_pre[l])
        qa, ka, va, pu, qc, kc, vc = _inproj(x2, g_pre, w_qkv, cq, sq, ck, sk, ones_bd, seq)
        ya = _global_attention(qa, ka, va, seq)
        yb = _pool_mixer(pu, w_pool[l].astype(BF16), row2(pool_scale[l]), seq)
        yc = _neighbourhood_attention(qc, kc, vc, _natten_bias(rpb[l]), seq)
        x2 = _merge(x2, ya, yb, yc, g_pre, w_gate, row2(b_gate[l]), wb, w_out[l].astype(BF16),
                    row2(norm_mix_post[l]))
        x2 = _ffn(x2, row2(norm_ffn_pre[l]), wup, cw, cb, wdn, row2(norm_ffn_post[l]), seq)
    return x2.reshape(b, seq, d)
```

```python
import functools

import numpy as np
import jax
import jax.numpy as jnp
from jax import lax
from jax.experimental import pallas as pl
from jax.experimental.pallas import tpu as pltpu

F32 = jnp.float32
BF16 = jnp.bfloat16

D_MODEL = 1024
GRID_W = 64
HEAD_DIM = 64
A_HEADS = 8
A_KV_HEADS = 2
ROPE_THETA = 10000.0
POOL_WINDOWS = (2, 4, 8, 16)
POOL_GROUP = 128
C_HEADS = 8
NA_ROWS = 8
NA_COLS = 16
N_BRANCH = 3
BRANCH_WIDTH = 512
D_FF = 2816
EPS = 1e-6
QK_SCALE = HEAD_DIM ** -0.5

LANES = 128
V7X_VMEM_BYTES = 64 * 1024 * 1024
VMEM_LIMIT = 56 * 1024 * 1024

TOKEN_TILE = 512
ATTN_Q_TILE = 256
POOL_ROW_CHUNK = 256
POOL_HALO = 16
FF_CHUNK = 256
MASK_VALUE = -1e30
NA_ROW_UNROLL = 2

OFF_QA = 0
OFF_KVA = 512
OFF_POOL = 768
OFF_QC = 1280
OFF_KC = 1792
OFF_VC = 2304
W_QKV = 2816


def _params(n_grid, flags=None):
    return pltpu.CompilerParams(
        dimension_semantics=("arbitrary",) * n_grid, vmem_limit_bytes=VMEM_LIMIT, flags=flags)


def _resident(shape):
    zeros = (0,) * len(shape)
    return pl.BlockSpec(shape, lambda *_: zeros, pipeline_mode=pl.Buffered(1))


def _rms(x):
    return x * lax.rsqrt(jnp.mean(x * x, axis=-1, keepdims=True) + EPS)


def _dot(a, b):
    return jnp.dot(a, b, preferred_element_type=F32)


def _dot_nt(a, b):
    return lax.dot_general(a, b, (((1,), (1,)), ((), ())), preferred_element_type=F32)


def _inproj_kernel(x_ref, g_ref, w_ref, cq_ref, sq_ref, ck_ref, sk_ref, ones_ref,
                   qa_ref, ka_ref, va_ref, pu_ref, qc_ref, kc_ref, vc_ref,
                   *, tm, tiles_per_seq):
    i = pl.program_id(0)
    h = (_rms(x_ref[...]) * g_ref[...]).astype(BF16)
    row0 = pl.multiple_of((i % tiles_per_seq) * tm, tm)
    lane = lax.broadcasted_iota(jnp.int32, (tm, LANES), 1)
    first_half = (lane % 32) < 16

    def norm_rope(u, c_ref, s_ref):
        u2 = u * u
        hi = u2.astype(BF16)
        lo = (u2 - hi.astype(F32)).astype(BF16)
        ms = _dot(jnp.concatenate([hi, lo], axis=1), ones_ref[...])
        partner = jnp.where(first_half, pltpu.roll(u, LANES - 16, 1), pltpu.roll(u, 16, 1))
        c = c_ref[pl.ds(row0, tm), :]
        s = s_ref[pl.ds(row0, tm), :]
        return (u * c + partner * s) * lax.rsqrt(ms + EPS)

    qa = _dot(h, w_ref[:, OFF_QA:OFF_QA + 512])
    for c in range(4):
        sl = slice(c * LANES, (c + 1) * LANES)
        qa_ref[:, sl] = norm_rope(qa[:, sl], cq_ref, sq_ref).astype(BF16)
    kv = _dot(h, w_ref[:, OFF_KVA:OFF_KVA + 256])
    ka_ref[...] = norm_rope(kv[:, :LANES], ck_ref, sk_ref).astype(BF16)
    va_ref[...] = kv[:, LANES:].astype(BF16)
    pu_ref[...] = _dot(h, w_ref[:, OFF_POOL:OFF_POOL + 512])
    qc_ref[...] = (_dot(h, w_ref[:, OFF_QC:OFF_QC + 512]) * QK_SCALE).astype(BF16)
    kc_ref[...] = _dot(h, w_ref[:, OFF_KC:OFF_KC + 512]).astype(BF16)
    vc_ref[...] = _dot(h, w_ref[:, OFF_VC:OFF_VC + 512]).astype(BF16)


def _inproj(x2, g_pre, w_qkv, cq, sq, ck, sk, ones_bd, seq):
    t = x2.shape[0]
    tm = TOKEN_TILE
    tiles_per_seq = seq // tm
    row = lambda width: pl.BlockSpec((tm, width), lambda i: (i, 0))
    out_widths = (512, LANES, LANES, 512, 512, 512, 512)
    out_dtypes = (BF16, BF16, BF16, F32, BF16, BF16, BF16)
    return pl.pallas_call(
        functools.partial(_inproj_kernel, tm=tm, tiles_per_seq=tiles_per_seq),
        grid=(t // tm,),
        in_specs=[row(D_MODEL), _resident((1, D_MODEL)), _resident((D_MODEL, W_QKV)),
                  _resident((seq, LANES)), _resident((seq, LANES)),
                  _resident((seq, LANES)), _resident((seq, LANES)),
                  _resident((2 * LANES, LANES))],
        out_specs=[row(w) for w in out_widths],
        out_shape=[jax.ShapeDtypeStruct((t, w), d) for w, d in zip(out_widths, out_dtypes)],
        compiler_params=_params(1),
        name="inproj",
    )(x2, g_pre, w_qkv, cq, sq, ck, sk, ones_bd)


def _split_stack(x, lo):
    zero = jnp.zeros_like(x)
    return jnp.concatenate([jnp.where(lo, x, zero), jnp.where(lo, zero, x)], axis=0)


def _softmax_pv(s, v):
    m = jnp.max(s, axis=-1, keepdims=True)
    p = jnp.exp(s - m)
    l = jnp.sum(p, axis=-1, keepdims=True)
    return _dot(p.astype(BF16), v) / l


def _attn_kernel(q_ref, k_ref, v_ref, o_ref, s_a, s_b, p_a, p_b, *, tq):
    n_chunks = 4
    seq = k_ref.shape[0]
    lo = lax.broadcasted_iota(jnp.int32, (tq, LANES), 1) < HEAD_DIM
    ones = jnp.ones((seq, LANES), BF16)
    lanes = lambda c: slice(c * LANES, (c + 1) * LANES)
    s_bufs = (s_a, s_b)
    p_bufs = (p_a, p_b)

    def score(c):
        s_bufs[c % 2][...] = _dot_nt(_split_stack(q_ref[:, lanes(c)], lo), k_ref[...])

    def softmax(c):
        s = s_bufs[c % 2][...]
        p_bufs[c % 2][...] = jnp.exp(s - jnp.max(s, axis=-1, keepdims=True)).astype(BF16)

    def value(c):
        v_aug = jnp.concatenate([v_ref[...], ones], axis=1)
        pv = _dot(p_bufs[c % 2][...], v_aug)
        pv = pv[:, :LANES] / pv[:, LANES:]
        o_ref[:, lanes(c)] = jnp.where(lo, pv[:tq], pv[tq:]).astype(o_ref.dtype)

    for t in range(n_chunks + 2):
        if t < n_chunks:
            score(t)
        if 0 <= t - 1 < n_chunks:
            softmax(t - 1)
        if 0 <= t - 2 < n_chunks:
            value(t - 2)


def _global_attention(qa, ka, va, seq):
    t = qa.shape[0]
    tq = ATTN_Q_TILE
    nq = seq // tq
    return pl.pallas_call(
        functools.partial(_attn_kernel, tq=tq),
        grid=(t // seq, nq),
        in_specs=[pl.BlockSpec((tq, 512), lambda b, j: (b * nq + j, 0)),
                  pl.BlockSpec((seq, LANES), lambda b, j: (b, 0)),
                  pl.BlockSpec((seq, LANES), lambda b, j: (b, 0))],
        out_specs=pl.BlockSpec((tq, 512), lambda b, j: (b * nq + j, 0)),
        out_shape=jax.ShapeDtypeStruct((t, 512), BF16),
        scratch_shapes=[pltpu.VMEM((2 * tq, seq), F32), pltpu.VMEM((2 * tq, seq), F32),
                        pltpu.VMEM((2 * tq, seq), BF16), pltpu.VMEM((2 * tq, seq), BF16)],
        compiler_params=_params(2),
        name="global_attention",
    )(qa, ka, va)


def _pool_kernel(u_ref, w_ref, scale_ref, o_ref, pad_ref, *, seq):
    width = u_ref.shape[1]
    zeros = jnp.zeros((POOL_HALO, width), F32)
    pad_ref[0:POOL_HALO, :] = zeros
    pad_ref[POOL_HALO + seq:, :] = zeros
    pad_ref[POOL_HALO:POOL_HALO + seq, :] = u_ref[...]
    rc = POOL_ROW_CHUNK
    for r0 in range(0, seq, rc):
        t = r0 + lax.broadcasted_iota(jnp.int32, (rc, POOL_GROUP), 0)
        for g, w in enumerate(POOL_WINDOWS):
            sl = slice(g * POOL_GROUP, (g + 1) * POOL_GROUP)
            half = w // 2
            acc = pad_ref[POOL_HALO + r0 - half:POOL_HALO + r0 - half + rc, sl]
            for d in range(-half + 1, w - half):
                acc = acc + pad_ref[POOL_HALO + r0 + d:POOL_HALO + r0 + d + rc, sl]
            cnt = jnp.minimum(t + (w - half), seq) - jnp.maximum(t - half, 0)
            mean = acc / cnt.astype(F32)
            diff = (mean - u_ref[r0:r0 + rc, sl]).astype(BF16)
            y = _dot(diff, w_ref[g]) * scale_ref[:, sl]
            o_ref[r0:r0 + rc, sl] = y.astype(o_ref.dtype)


def _pool_mixer(pu, w_pool, pool_scale, seq):
    t, width = pu.shape
    n_g = len(POOL_WINDOWS)
    return pl.pallas_call(
        functools.partial(_pool_kernel, seq=seq),
        grid=(t // seq,),
        in_specs=[pl.BlockSpec((seq, width), lambda b: (b, 0)),
                  _resident((n_g, POOL_GROUP, POOL_GROUP)),
                  _resident((1, width))],
        out_specs=pl.BlockSpec((seq, width), lambda b: (b, 0)),
        out_shape=jax.ShapeDtypeStruct((t, width), BF16),
        scratch_shapes=[pltpu.VMEM((seq + 2 * POOL_HALO, width), F32)],
        compiler_params=_params(1),
        name="pool_mixer",
    )(pu, w_pool, pool_scale)


def _natten_kernel(q_ref, k_ref, v_ref, bias_ref, o_ref, s_a, s_b, p_a, p_b, *, rows):
    win = NA_ROWS * GRID_W
    n_pairs = C_HEADS // 2
    n_steps = rows // NA_ROW_UNROLL
    lo = lax.broadcasted_iota(jnp.int32, (GRID_W, LANES), 1) < HEAD_DIM
    ones = jnp.ones((win, LANES), BF16)
    lanes = lambda pr: slice(pr * LANES, (pr + 1) * LANES)
    s_bufs = (s_a, s_b)
    p_bufs = (p_a, p_b)

    def jobs_of(step):
        jobs = []
        for u in range(NA_ROW_UNROLL):
            r = step * NA_ROW_UNROLL + u
            rs = jnp.clip(r - NA_ROWS // 2, 0, rows - NA_ROWS)
            kind = rs - r + (NA_ROWS - 1)
            q0 = pl.multiple_of(r * GRID_W, GRID_W)
            k0 = pl.multiple_of(rs * GRID_W, GRID_W)
            jobs.extend((u * n_pairs + pr, pr, kind, q0, k0) for pr in range(n_pairs))
        return jobs

    def score_job(par, n, pr, kind, q0, k0):
        qs = _split_stack(q_ref[pl.ds(q0, GRID_W), lanes(pr)], lo)
        s_bufs[par][n] = _dot_nt(qs, k_ref[pl.ds(k0, win), lanes(pr)]) + bias_ref[kind, pr]

    def softmax_job(par, n, pr, kind, q0, k0):
        s = s_bufs[par][n]
        p_bufs[par][n] = jnp.exp(s - jnp.max(s, axis=-1, keepdims=True)).astype(BF16)

    def value_job(par, n, pr, kind, q0, k0):
        v_aug = jnp.concatenate([v_ref[pl.ds(k0, win), lanes(pr)], ones], axis=1)
        pv = _dot(p_bufs[par][n], v_aug)
        pv = pv[:, :LANES] / pv[:, LANES:]
        o_ref[pl.ds(q0, GRID_W), lanes(pr)] = jnp.where(
            lo, pv[:GRID_W], pv[GRID_W:]).astype(o_ref.dtype)

    def emit(score_step=None, softmax_step=None, value_step=None):
        plan = []
        for step, par, fn in (score_step, softmax_step, value_step):
            plan.append([(fn, par, job) for job in jobs_of(step)] if step is not None else [])
        for n in range(NA_ROW_UNROLL * n_pairs):
            for stage in plan:
                if stage:
                    fn, par, job = stage[n]
                    fn(par, *job)

    sc = lambda step, par: (step, par, score_job)
    sm = lambda step, par: (step, par, softmax_job)
    va = lambda step, par: (step, par, value_job)
    none = (None, None, None)

    emit(sc(0, 0), none, none)
    emit(sc(1, 1), sm(0, 0), none)

    def body(m, carry):
        i = 2 * m + 1
        emit(sc(i + 1, 0), sm(i, 1), va(i - 1, 0))
        emit(sc(i + 2, 1), sm(i + 1, 0), va(i, 1))
        return carry

    lax.fori_loop(0, (n_steps - 2) // 2, body, 0)
    emit(none, sm(n_steps - 1, 1), va(n_steps - 2, 0))
    emit(none, none, va(n_steps - 1, 1))


def _neighbourhood_attention(qc, kc, vc, bias, seq):
    t, width = qc.shape
    rows = seq // GRID_W
    n_steps = rows // NA_ROW_UNROLL
    assert rows % NA_ROW_UNROLL == 0 and n_steps % 2 == 0 and n_steps >= 4
    n_jobs = NA_ROW_UNROLL * C_HEADS // 2
    blk = pl.BlockSpec((seq, width), lambda b: (b, 0))
    s_buf = pltpu.VMEM((n_jobs, 2 * GRID_W, NA_ROWS * GRID_W), F32)
    p_buf = pltpu.VMEM((n_jobs, 2 * GRID_W, NA_ROWS * GRID_W), BF16)
    return pl.pallas_call(
        functools.partial(_natten_kernel, rows=rows),
        grid=(t // seq,),
        in_specs=[blk, blk, blk, _resident(bias.shape)],
        out_specs=blk,
        out_shape=jax.ShapeDtypeStruct((t, width), BF16),
        scratch_shapes=[s_buf, s_buf, p_buf, p_buf],
        compiler_params=_params(1),
        name="neighbourhood_attention",
    )(qc, kc, vc, bias)


def _merge_kernel(x_ref, ya_ref, yb_ref, yc_ref, gpre_ref, wg_ref, bg_ref, wb_ref, wo_ref,
                  gpost_ref, o_ref):
    x = x_ref[...]
    h = (_rms(x) * gpre_ref[...]).astype(BF16)
    merged = None
    for n, y_ref in enumerate((ya_ref, yb_ref, yc_ref)):
        sl = slice(n * D_MODEL, (n + 1) * D_MODEL)
        gate = jax.nn.sigmoid(_dot(h, wg_ref[:, sl]) + bg_ref[:, sl])
        term = gate * _dot(y_ref[...], wb_ref[n])
        merged = term if merged is None else merged + term
    mix = _dot(merged.astype(BF16), wo_ref[...])
    o_ref[...] = x + _rms(mix) * gpost_ref[...]


def _merge(x2, ya, yb, yc, g_pre, w_gate, b_gate, w_branch, w_out, g_post):
    t = x2.shape[0]
    tm = TOKEN_TILE
    row = lambda width: pl.BlockSpec((tm, width), lambda i: (i, 0))
    return pl.pallas_call(
        _merge_kernel,
        grid=(t // tm,),
        in_specs=[row(D_MODEL), row(BRANCH_WIDTH), row(BRANCH_WIDTH), row(BRANCH_WIDTH),
                  _resident((1, D_MODEL)), _resident((D_MODEL, N_BRANCH * D_MODEL)),
                  _resident((1, N_BRANCH * D_MODEL)),
                  _resident((N_BRANCH, BRANCH_WIDTH, D_MODEL)),
                  _resident((D_MODEL, D_MODEL)), _resident((1, D_MODEL))],
        out_specs=row(D_MODEL),
        out_shape=jax.ShapeDtypeStruct((t, D_MODEL), F32),
        compiler_params=_params(1),
        name="merge",
    )(x2, ya, yb, yc, g_pre, w_gate, b_gate, w_branch, w_out, g_post)


def _ffn_kernel(x_ref, xp_ref, xn_ref, gpre_ref, wup_ref, cw_ref, cb_ref, wdn_ref, gpost_ref,
                o_ref, h_ref, up_a, up_b, act_a, act_b, acc_ref, *, tm, tiles_per_seq, n_chunks):
    i = pl.program_id(0)
    pos = i % tiles_per_seq
    has_prev = (pos > 0).astype(F32)
    has_next = (pos < tiles_per_seq - 1).astype(F32)
    xs = jnp.concatenate([xp_ref[...] * has_prev, x_ref[...], xn_ref[...] * has_next], axis=0)
    h_ref[...] = (_rms(xs) * gpre_ref[...]).astype(BF16)
    ups = (up_a, up_b)
    acts = (act_a, act_b)

    row = lax.broadcasted_iota(jnp.int32, (tm, LANES), 0)
    is_first = row == 0
    is_last = row == tm - 1

    def up_job(j, par, half):
        ups[par][half] = _dot(h_ref[...], wup_ref[half, j])

    def conv(up_ref, half, j, sl):
        main = up_ref[half, 8:8 + tm, sl]
        before = jnp.where(is_first, up_ref[half, 7:8, sl], pltpu.roll(main, 1, 0))
        after = jnp.where(is_last, up_ref[half, 8 + tm:9 + tm, sl], pltpu.roll(main, tm - 1, 0))
        w = lambda k: cw_ref[k, half, j][:, sl]
        return before * w(0) + main * w(1) + after * w(2) + cb_ref[half, j][:, sl]

    def act_job(j, par, lane_half):
        sl = slice(lane_half * LANES, (lane_half + 1) * LANES)
        val = conv(ups[par], 0, j, sl)
        gate = conv(ups[par], 1, j, sl)
        acts[par][:, sl] = (jax.nn.gelu(gate, approximate=True) * val).astype(BF16)

    def down_job(j, par, blk, first):
        sl = slice(blk * FF_CHUNK, (blk + 1) * FF_CHUNK)
        part = _dot(acts[par][...], wdn_ref[j][:, sl])
        if first:
            acc_ref[:, sl] = part
        else:
            acc_ref[:, sl] += part

    def step(up=None, act=None, down=None, first=False):
        u = [functools.partial(up_job, *up, half) for half in range(2)] if up else []
        a = [functools.partial(act_job, *act, lh) for lh in range(2)] if act else []
        d = [functools.partial(down_job, *down, blk, first)
             for blk in range(D_MODEL // FF_CHUNK)] if down else []
        for group in (u[:1], a[:1], d[:2], u[1:], a[1:], d[2:]):
            for job in group:
                job()

    step(up=(0, 0))
    step(up=(1, 1), act=(0, 0))
    step(up=(2, 0), act=(1, 1), down=(0, 0), first=True)

    def pair(m, carry):
        j = 2 * m
        step(up=(j + 1, 1), act=(j, 0), down=(j - 1, 1))
        step(up=(j + 2, 0), act=(j + 1, 1), down=(j, 0))
        return carry

    lax.fori_loop(1, (n_chunks - 1) // 2, pair, 0)
    step(act=(n_chunks - 1, 0), down=(n_chunks - 2, 1))
    step(down=(n_chunks - 1, 0))
    o_ref[...] = x_ref[...] + _rms(acc_ref[...]) * gpost_ref[...]


def _ffn(x2, g_pre, w_up, conv_w, conv_b, w_down, g_post, seq):
    t = x2.shape[0]
    tm = TOKEN_TILE
    tiles_per_seq = seq // tm
    n_chunks = D_FF // FF_CHUNK
    sub = tm // 8
    last = t // 8 - 1
    return pl.pallas_call(
        functools.partial(_ffn_kernel, tm=tm, tiles_per_seq=tiles_per_seq, n_chunks=n_chunks),
        grid=(t // tm,),
        in_specs=[pl.BlockSpec((tm, D_MODEL), lambda i: (i, 0)),
                  pl.BlockSpec((8, D_MODEL), lambda i: (jnp.maximum(i * sub - 1, 0), 0)),
                  pl.BlockSpec((8, D_MODEL), lambda i: (jnp.minimum((i + 1) * sub, last), 0)),
                  _resident((1, D_MODEL)),
                  _resident((2, n_chunks, D_MODEL, FF_CHUNK)),
                  _resident((3, 2, n_chunks, 1, FF_CHUNK)),
                  _resident((2, n_chunks, 1, FF_CHUNK)),
                  _resident((n_chunks, FF_CHUNK, D_MODEL)),
                  _resident((1, D_MODEL))],
        out_specs=pl.BlockSpec((tm, D_MODEL), lambda i: (i, 0)),
        out_shape=jax.ShapeDtypeStruct((t, D_MODEL), F32),
        scratch_shapes=[pltpu.VMEM((tm + 16, D_MODEL), BF16),
                        pltpu.VMEM((2, tm + 16, FF_CHUNK), F32),
                        pltpu.VMEM((2, tm + 16, FF_CHUNK), F32),
                        pltpu.VMEM((tm, FF_CHUNK), BF16),
                        pltpu.VMEM((tm, FF_CHUNK), BF16),
                        pltpu.VMEM((tm, D_MODEL), F32)],
        compiler_params=_params(1),
        name="conv_gated_mlp",
    )(x2, x2, x2, g_pre, w_up, conv_w, conv_b, w_down, g_post)


def _qa_column_order():
    cols = []
    for c in range(A_HEADS // A_KV_HEADS):
        cols.append(np.arange(c * HEAD_DIM, (c + 1) * HEAD_DIM))
        cols.append(np.arange((4 + c) * HEAD_DIM, (5 + c) * HEAD_DIM))
    return np.concatenate(cols)


def _rope_tables(gain, scale, seq):
    t = jnp.arange(seq)
    half = HEAD_DIM // 2
    freqs = ROPE_THETA ** (-jnp.arange(0, half, 2, dtype=F32) / half)

    def axis_tables(pos):
        ang = pos.astype(F32)[:, None] * freqs[None, :]
        c, s = jnp.cos(ang), jnp.sin(ang)
        return jnp.concatenate([c, c], -1), jnp.concatenate([-s, s], -1)

    cr, sr = axis_tables(t // GRID_W)
    cc, sc = axis_tables(t % GRID_W)
    cos = jnp.concatenate([cr, cc], -1)
    sin = jnp.concatenate([sr, sc], -1)
    j = np.arange(HEAD_DIM)
    partner = np.where((j % half) < half // 2, j + half // 2, j - half // 2)
    g = gain.astype(F32)
    ctab = cos * (g * scale)[None, :]
    stab = sin * (g[partner] * scale)[None, :]
    return jnp.tile(ctab, (1, 2)), jnp.tile(stab, (1, 2))


def _natten_bias(rpb):
    c = np.arange(GRID_W)
    cs = np.clip(c - NA_COLS // 2, 0, GRID_W - NA_COLS)
    kc = np.arange(GRID_W)
    dc = kc[None, :] - c[:, None] + (NA_COLS - 1)
    valid = (kc[None, :] >= cs[:, None]) & (kc[None, :] < cs[:, None] + NA_COLS)
    row_t = jnp.where(valid[None, None], rpb.astype(F32)[:, :, np.clip(dc, 0, 2 * NA_COLS - 2)],
                      MASK_VALUE)
    dr = np.arange(NA_ROWS)[:, None] + np.arange(NA_ROWS)[None, :]
    tab = row_t[:, dr]
    tab = tab.transpose(1, 0, 3, 2, 4)
    return tab.reshape(NA_ROWS, C_HEADS // 2, 2 * GRID_W, NA_ROWS * GRID_W)


def kernel(x, norm_mix_pre, norm_mix_post, norm_ffn_pre, norm_ffn_post, w_in, b_gate,
           qk_norm_q, qk_norm_k, w_pool, pool_scale, rpb, w_branch, w_out,
           w_up, conv_w, conv_b, w_down):
    b, seq, d = x.shape
    depth = w_in.shape[0]
    assert d == D_MODEL and seq % TOKEN_TILE == 0 and seq % ATTN_Q_TILE == 0
    assert seq // GRID_W >= NA_ROWS and seq % POOL_ROW_CHUNK == 0 and D_FF % FF_CHUNK == 0
    n_chunks = D_FF // FF_CHUNK
    assert n_chunks % 2 == 1 and n_chunks >= 3
    order = _qa_column_order()
    hh = np.arange(2 * LANES) % LANES // HEAD_DIM
    ones_bd = jnp.asarray((hh[:, None] == np.arange(LANES)[None, :] // HEAD_DIM) / HEAD_DIM, BF16)
    row2 = lambda v: v.astype(F32).reshape(1, -1)

    x2 = x.reshape(b * seq, d)
    for l in range(depth):
        w_l = w_in[l]
        w_qkv = jnp.concatenate([w_l[:, :512][:, order], w_l[:, 512:W_QKV]], axis=1).astype(BF16)
        w_gate = w_l[:, W_QKV:].astype(BF16)
        cq, sq = _rope_tables(qk_norm_q[l], QK_SCALE, seq)
        ck, sk = _rope_tables(qk_norm_k[l], 1.0, seq)
        wb = w_branch[l]
        wb = jnp.concatenate([wb[0][order][None], wb[1:]], axis=0).astype(BF16)
        wup = w_up[l].reshape(D_MODEL, 2, n_chunks, FF_CHUNK).transpose(1, 2, 0, 3).astype(BF16)
        cw = conv_w[l].astype(F32).reshape(3, 2, n_chunks, 1, FF_CHUNK)
        cb = conv_b[l].astype(F32).reshape(2, n_chunks, 1, FF_CHUNK)
        wdn = w_down[l].reshape(n_chunks, FF_CHUNK, D_MODEL).astype(BF16)

        g_pre = row2(norm_mix_pre[l])
        qa, ka, va, pu, qc, kc, vc = _inproj(x2, g_pre, w_qkv, cq, sq, ck, sk, ones_bd, seq)
        ya = _global_attention(qa, ka, va, seq)
        yb = _pool_mixer(pu, w_pool[l].astype(BF16), row2(pool_scale[l]), seq)
        yc = _neighbourhood_attention(qc, kc, vc, _natten_bias(rpb[l]), seq)
        x2 = _merge(x2, ya, yb, yc, g_pre, w_gate, row2(b_gate[l]), wb, w_out[l].astype(BF16),
                    row2(norm_mix_post[l]))
        x2 = _ffn(x2, row2(norm_ffn_pre[l]), wup, cw, cb, wdn, row2(norm_ffn_post[l]), seq)
    return x2.reshape(b, seq, d)
```

```python
import functools

import numpy as np
import jax
import jax.numpy as jnp
from jax import lax
from jax.experimental import pallas as pl
from jax.experimental.pallas import tpu as pltpu

F32 = jnp.float32
BF16 = jnp.bfloat16

D_MODEL = 1024
GRID_W = 64
HEAD_DIM = 64
A_HEADS = 8
A_KV_HEADS = 2
ROPE_THETA = 10000.0
POOL_WINDOWS = (2, 4, 8, 16)
POOL_GROUP = 128
C_HEADS = 8
NA_ROWS = 8
NA_COLS = 16
N_BRANCH = 3
BRANCH_WIDTH = 512
D_FF = 2816
EPS = 1e-6
QK_SCALE = HEAD_DIM ** -0.5
GELU_C = float(np.sqrt(2.0 / np.pi))
GELU_A = 0.044715
LOG2_E = float(np.log2(np.e))

LANES = 128
V7X_VMEM_BYTES = 64 * 1024 * 1024
VMEM_LIMIT = 56 * 1024 * 1024

TOKEN_TILE = 512
ATTN_Q_TILE = 256
POOL_ROW_CHUNK = 256
POOL_HALO = 16
FF_CHUNK = 256
FF_HALO = 16
FF_UP_JOBS = 2
FF_ACT_ROW_JOBS = 4
FF_DOWN_ROW_JOBS = 4
MASK_VALUE = -1e30
NA_ROW_UNROLL = 2

OFF_QA = 0
OFF_KVA = 512
OFF_POOL = 768
OFF_QC = 1280
OFF_KC = 1792
OFF_VC = 2304
W_QKV = 2816


def _params(n_grid, flags=None):
    return pltpu.CompilerParams(
        dimension_semantics=("arbitrary",) * n_grid, vmem_limit_bytes=VMEM_LIMIT, flags=flags)


def _resident(shape):
    zeros = (0,) * len(shape)
    return pl.BlockSpec(shape, lambda *_: zeros, pipeline_mode=pl.Buffered(1))


def _rms(x):
    return x * lax.rsqrt(jnp.mean(x * x, axis=-1, keepdims=True) + EPS)


def _dot(a, b):
    return jnp.dot(a, b, preferred_element_type=F32)


def _dot_nt(a, b):
    return lax.dot_general(a, b, (((1,), (1,)), ((), ())), preferred_element_type=F32)


def _inproj_kernel(x_ref, g_ref, w_ref, cq_ref, sq_ref, ck_ref, sk_ref, ones_ref,
                   qa_ref, ka_ref, va_ref, pu_ref, qc_ref, kc_ref, vc_ref,
                   *, tm, tiles_per_seq):
    i = pl.program_id(0)
    h = (_rms(x_ref[...]) * g_ref[...]).astype(BF16)
    row0 = pl.multiple_of((i % tiles_per_seq) * tm, tm)
    lane = lax.broadcasted_iota(jnp.int32, (tm, LANES), 1)
    first_half = (lane % 32) < 16

    def norm_rope(u, c_ref, s_ref):
        u2 = u * u
        hi = u2.astype(BF16)
        lo = (u2 - hi.astype(F32)).astype(BF16)
        ms = _dot(jnp.concatenate([hi, lo], axis=1), ones_ref[...])
        partner = jnp.where(first_half, pltpu.roll(u, LANES - 16, 1), pltpu.roll(u, 16, 1))
        c = c_ref[pl.ds(row0, tm), :]
        s = s_ref[pl.ds(row0, tm), :]
        return (u * c + partner * s) * lax.rsqrt(ms + EPS)

    qa = _dot(h, w_ref[:, OFF_QA:OFF_QA + 512])
    for c in range(4):
        sl = slice(c * LANES, (c + 1) * LANES)
        qa_ref[:, sl] = norm_rope(qa[:, sl], cq_ref, sq_ref).astype(BF16)
    kv = _dot(h, w_ref[:, OFF_KVA:OFF_KVA + 256])
    ka_ref[...] = norm_rope(kv[:, :LANES], ck_ref, sk_ref).astype(BF16)
    va_ref[...] = kv[:, LANES:].astype(BF16)
    pu_ref[...] = _dot(h, w_ref[:, OFF_POOL:OFF_POOL + 512])
    qc_ref[...] = (_dot(h, w_ref[:, OFF_QC:OFF_QC + 512]) * QK_SCALE).astype(BF16)
    kc_ref[...] = _dot(h, w_ref[:, OFF_KC:OFF_KC + 512]).astype(BF16)
    vc_ref[...] = _dot(h, w_ref[:, OFF_VC:OFF_VC + 512]).astype(BF16)


def _inproj(x2, g_pre, w_qkv, cq, sq, ck, sk, ones_bd, seq):
    t = x2.shape[0]
    tm = TOKEN_TILE
    tiles_per_seq = seq // tm
    row = lambda width: pl.BlockSpec((tm, width), lambda i: (i, 0))
    out_widths = (512, LANES, LANES, 512, 512, 512, 512)
    out_dtypes = (BF16, BF16, BF16, F32, BF16, BF16, BF16)
    return pl.pallas_call(
        functools.partial(_inproj_kernel, tm=tm, tiles_per_seq=tiles_per_seq),
        grid=(t // tm,),
        in_specs=[row(D_MODEL), _resident((1, D_MODEL)), _resident((D_MODEL, W_QKV)),
                  _resident((seq, LANES)), _resident((seq, LANES)),
                  _resident((seq, LANES)), _resident((seq, LANES)),
                  _resident((2 * LANES, LANES))],
        out_specs=[row(w) for w in out_widths],
        out_shape=[jax.ShapeDtypeStruct((t, w), d) for w, d in zip(out_widths, out_dtypes)],
        compiler_params=_params(1),
        name="inproj",
    )(x2, g_pre, w_qkv, cq, sq, ck, sk, ones_bd)


def _split_stack(x, lo):
    zero = jnp.zeros_like(x)
    return jnp.concatenate([jnp.where(lo, x, zero), jnp.where(lo, zero, x)], axis=0)


def _softmax_pv(s, v):
    m = jnp.max(s, axis=-1, keepdims=True)
    p = jnp.exp(s - m)
    l = jnp.sum(p, axis=-1, keepdims=True)
    return _dot(p.astype(BF16), v) / l


def _attn_kernel(q_ref, k_ref, v_ref, o_ref, s_a, s_b, p_a, p_b, *, tq):
    n_chunks = 4
    seq = k_ref.shape[0]
    lo = lax.broadcasted_iota(jnp.int32, (tq, LANES), 1) < HEAD_DIM
    ones = jnp.ones((seq, LANES), BF16)
    lanes = lambda c: slice(c * LANES, (c + 1) * LANES)
    s_bufs = (s_a, s_b)
    p_bufs = (p_a, p_b)

    def score(c):
        s_bufs[c % 2][...] = _dot_nt(_split_stack(q_ref[:, lanes(c)], lo), k_ref[...])

    def softmax(c):
        s = s_bufs[c % 2][...]
        p_bufs[c % 2][...] = jnp.exp(s - jnp.max(s, axis=-1, keepdims=True)).astype(BF16)

    def value(c):
        v_aug = jnp.concatenate([v_ref[...], ones], axis=1)
        pv = _dot(p_bufs[c % 2][...], v_aug)
        pv = pv[:, :LANES] / pv[:, LANES:]
        o_ref[:, lanes(c)] = jnp.where(lo, pv[:tq], pv[tq:]).astype(o_ref.dtype)

    for t in range(n_chunks + 2):
        if t < n_chunks:
            score(t)
        if 0 <= t - 1 < n_chunks:
            softmax(t - 1)
        if 0 <= t - 2 < n_chunks:
            value(t - 2)


def _global_attention(qa, ka, va, seq):
    t = qa.shape[0]
    tq = ATTN_Q_TILE
    nq = seq // tq
    return pl.pallas_call(
        functools.partial(_attn_kernel, tq=tq),
        grid=(t // seq, nq),
        in_specs=[pl.BlockSpec((tq, 512), lambda b, j: (b * nq + j, 0)),
                  pl.BlockSpec((seq, LANES), lambda b, j: (b, 0)),
                  pl.BlockSpec((seq, LANES), lambda b, j: (b, 0))],
        out_specs=pl.BlockSpec((tq, 512), lambda b, j: (b * nq + j, 0)),
        out_shape=jax.ShapeDtypeStruct((t, 512), BF16),
        scratch_shapes=[pltpu.VMEM((2 * tq, seq), F32), pltpu.VMEM((2 * tq, seq), F32),
                        pltpu.VMEM((2 * tq, seq), BF16), pltpu.VMEM((2 * tq, seq), BF16)],
        compiler_params=_params(2),
        name="global_attention",
    )(qa, ka, va)


def _pool_kernel(u_ref, w_ref, scale_ref, o_ref, pad_ref, *, seq):
    width = u_ref.shape[1]
    zeros = jnp.zeros((POOL_HALO, width), F32)
    pad_ref[0:POOL_HALO, :] = zeros
    pad_ref[POOL_HALO + seq:, :] = zeros
    pad_ref[POOL_HALO:POOL_HALO + seq, :] = u_ref[...]
    rc = POOL_ROW_CHUNK
    for r0 in range(0, seq, rc):
        t = r0 + lax.broadcasted_iota(jnp.int32, (rc, POOL_GROUP), 0)
        for g, w in enumerate(POOL_WINDOWS):
            sl = slice(g * POOL_GROUP, (g + 1) * POOL_GROUP)
            half = w // 2
            acc = pad_ref[POOL_HALO + r0 - half:POOL_HALO + r0 - half + rc, sl]
            for d in range(-half + 1, w - half):
                acc = acc + pad_ref[POOL_HALO + r0 + d:POOL_HALO + r0 + d + rc, sl]
            cnt = jnp.minimum(t + (w - half), seq) - jnp.maximum(t - half, 0)
            mean = acc / cnt.astype(F32)
            diff = (mean - u_ref[r0:r0 + rc, sl]).astype(BF16)
            y = _dot(diff, w_ref[g]) * scale_ref[:, sl]
            o_ref[r0:r0 + rc, sl] = y.astype(o_ref.dtype)


def _pool_mixer(pu, w_pool, pool_scale, seq):
    t, width = pu.shape
    n_g = len(POOL_WINDOWS)
    return pl.pallas_call(
        functools.partial(_pool_kernel, seq=seq),
        grid=(t // seq,),
        in_specs=[pl.BlockSpec((seq, width), lambda b: (b, 0)),
                  _resident((n_g, POOL_GROUP, POOL_GROUP)),
                  _resident((1, width))],
        out_specs=pl.BlockSpec((seq, width), lambda b: (b, 0)),
        out_shape=jax.ShapeDtypeStruct((t, width), BF16),
        scratch_shapes=[pltpu.VMEM((seq + 2 * POOL_HALO, width), F32)],
        compiler_params=_params(1),
        name="pool_mixer",
    )(pu, w_pool, pool_scale)


def _natten_kernel(q_ref, k_ref, v_ref, bias_ref, o_ref, s_a, s_b, p_a, p_b, *, rows):
    win = NA_ROWS * GRID_W
    n_pairs = C_HEADS // 2
    n_steps = rows // NA_ROW_UNROLL
    lo = lax.broadcasted_iota(jnp.int32, (GRID_W, LANES), 1) < HEAD_DIM
    ones = jnp.ones((win, LANES), BF16)
    lanes = lambda pr: slice(pr * LANES, (pr + 1) * LANES)
    s_bufs = (s_a, s_b)
    p_bufs = (p_a, p_b)

    def jobs_of(step):
        jobs = []
        for u in range(NA_ROW_UNROLL):
            r = step * NA_ROW_UNROLL + u
            rs = min(max(r - NA_ROWS // 2, 0), rows - NA_ROWS)
            kind = rs - r + (NA_ROWS - 1)
            q0 = r * GRID_W
            k0 = rs * GRID_W
            jobs.extend((u * n_pairs + pr, pr, kind, q0, k0) for pr in range(n_pairs))
        return jobs

    def score_job(par, n, pr, kind, q0, k0):
        qs = _split_stack(q_ref[pl.ds(q0, GRID_W), lanes(pr)], lo)
        bias = jnp.concatenate([bias_ref[pr, kind + 2 * jj] for jj in range(NA_ROWS // 2)], axis=1)
        s_bufs[par][n] = _dot_nt(qs, k_ref[pl.ds(k0, win), lanes(pr)]) + bias

    def softmax_job(par, n, pr, kind, q0, k0):
        s = s_bufs[par][n]
        p_bufs[par][n] = jnp.exp(s - jnp.max(s, axis=-1, keepdims=True)).astype(BF16)

    def value_job(par, n, pr, kind, q0, k0):
        v_aug = jnp.concatenate([v_ref[pl.ds(k0, win), lanes(pr)], ones], axis=1)
        pv = _dot(p_bufs[par][n], v_aug)
        pv = pv[:, :LANES] / pv[:, LANES:]
        o_ref[pl.ds(q0, GRID_W), lanes(pr)] = jnp.where(
            lo, pv[:GRID_W], pv[GRID_W:]).astype(o_ref.dtype)

    def emit(score_step=None, softmax_step=None, value_step=None):
        plan = []
        for step, par, fn in (score_step, softmax_step, value_step):
            plan.append([(fn, par, job) for job in jobs_of(step)] if step is not None else [])
        for n in range(NA_ROW_UNROLL * n_pairs):
            for stage in plan:
                if stage:
                    fn, par, job = stage[n]
                    fn(par, *job)

    sc = lambda step, par: (step, par, score_job)
    sm = lambda step, par: (step, par, softmax_job)
    va = lambda step, par: (step, par, value_job)
    none = (None, None, None)

    emit(sc(0, 0), none, none)
    emit(sc(1, 1), sm(0, 0), none)

    def body(m, carry):
        i = 2 * m + 1
        emit(sc(i + 1, 0), sm(i, 1), va(i - 1, 0))
        emit(sc(i + 2, 1), sm(i + 1, 0), va(i, 1))
        return carry

    for m in range((n_steps - 2) // 2):
        body(m, 0)
    emit(none, sm(n_steps - 1, 1), va(n_steps - 2, 0))
    emit(none, none, va(n_steps - 1, 1))


def _neighbourhood_attention(qc, kc, vc, bias, seq):
    t, width = qc.shape
    rows = seq // GRID_W
    n_steps = rows // NA_ROW_UNROLL
    assert rows % NA_ROW_UNROLL == 0 and n_steps % 2 == 0 and n_steps >= 4
    n_jobs = NA_ROW_UNROLL * C_HEADS // 2
    blk = pl.BlockSpec((seq, width), lambda b: (b, 0))
    s_buf = pltpu.VMEM((n_jobs, 2 * GRID_W, NA_ROWS * GRID_W), F32)
    p_buf = pltpu.VMEM((n_jobs, 2 * GRID_W, NA_ROWS * GRID_W), BF16)
    return pl.pallas_call(
        functools.partial(_natten_kernel, rows=rows),
        grid=(t // seq,),
        in_specs=[blk, blk, blk, _resident(bias.shape)],
        out_specs=blk,
        out_shape=jax.ShapeDtypeStruct((t, width), BF16),
        scratch_shapes=[s_buf, s_buf, p_buf, p_buf],
        compiler_params=_params(1),
        name="neighbourhood_attention",
    )(qc, kc, vc, bias)


def _merge_kernel(x_ref, ya_ref, yb_ref, yc_ref, gpre_ref, wg_ref, bg_ref, wb_ref, wo_ref,
                  gpost_ref, o_ref):
    x = x_ref[...]
    h = (_rms(x) * gpre_ref[...]).astype(BF16)
    merged = None
    for n, y_ref in enumerate((ya_ref, yb_ref, yc_ref)):
        sl = slice(n * D_MODEL, (n + 1) * D_MODEL)
        gate = jax.nn.sigmoid(_dot(h, wg_ref[:, sl]) + bg_ref[:, sl])
        term = gate * _dot(y_ref[...], wb_ref[n])
        merged = term if merged is None else merged + term
    mix = _dot(merged.astype(BF16), wo_ref[...])
    o_ref[...] = x + _rms(mix) * gpost_ref[...]


def _merge(x2, ya, yb, yc, g_pre, w_gate, b_gate, w_branch, w_out, g_post):
    t = x2.shape[0]
    tm = TOKEN_TILE
    row = lambda width: pl.BlockSpec((tm, width), lambda i: (i, 0))
    return pl.pallas_call(
        _merge_kernel,
        grid=(t // tm,),
        in_specs=[row(D_MODEL), row(BRANCH_WIDTH), row(BRANCH_WIDTH), row(BRANCH_WIDTH),
                  _resident((1, D_MODEL)), _resident((D_MODEL, N_BRANCH * D_MODEL)),
                  _resident((1, N_BRANCH * D_MODEL)),
                  _resident((N_BRANCH, BRANCH_WIDTH, D_MODEL)),
                  _resident((D_MODEL, D_MODEL)), _resident((1, D_MODEL))],
        out_specs=row(D_MODEL),
        out_shape=jax.ShapeDtypeStruct((t, D_MODEL), F32),
        compiler_params=_params(1),
        name="merge",
    )(x2, ya, yb, yc, g_pre, w_gate, b_gate, w_branch, w_out, g_post)


def _ffn_kernel(x_ref, xp_ref, xn_ref, gpre_ref, wup_ref, cw_ref, cb_ref, wdn_ref, gpost_ref,
                o_ref, h_ref, up_a, up_b, act_a, act_b, acc_ref, *, tm, tiles_per_seq, n_chunks):
    i = pl.program_id(0)
    pos = i % tiles_per_seq
    halo = FF_HALO
    has_prev = (pos > 0).astype(F32)
    has_next = (pos < tiles_per_seq - 1).astype(F32)
    xs = jnp.concatenate([xp_ref[...] * has_prev, x_ref[...], xn_ref[...] * has_next], axis=0)
    h_ref[...] = (_rms(xs) * gpre_ref[...]).astype(BF16)
    ups = (up_a, up_b)
    acts = (act_a, act_b)

    up_rows = (tm + 2 * halo) // FF_UP_JOBS
    act_rows = tm // FF_ACT_ROW_JOBS
    down_rows = tm // FF_DOWN_ROW_JOBS

    def hidden(half, j, start, width):
        off = half * D_FF + j * FF_CHUNK + start
        return slice(off, off + width)

    def up_job(j, par, half, rb):
        rows = slice(rb * up_rows, (rb + 1) * up_rows)
        ups[par][half, rows] = _dot(h_ref[rows], wup_ref[:, hidden(half, j, 0, FF_CHUNK)])

    def conv(up_ref, half, j, r0, sl):
        top = halo + r0 - 8
        n_ext = act_rows + 16
        ext = up_ref[half, top:top + n_ext, sl]
        inner = lambda y: y[8:8 + act_rows]
        before = inner(pltpu.roll(ext, 1, 0))
        after = inner(pltpu.roll(ext, n_ext - 1, 0))
        cols = hidden(half, j, sl.start, LANES)
        w = lambda k: cw_ref[k:k + 1, cols]
        return before * w(0) + inner(ext) * w(1) + after * w(2) + cb_ref[:, cols]

    def gelu_tanh(g):
        k1 = -2.0 * GELU_C * LOG2_E
        return g / (1.0 + jnp.exp2(g * (k1 + (k1 * GELU_A) * (g * g))))

    def act_job(j, par, rb, lane_half):
        r0 = rb * act_rows
        sl = slice(lane_half * LANES, (lane_half + 1) * LANES)
        val = conv(ups[par], 0, j, r0, sl)
        gate = conv(ups[par], 1, j, r0, sl)
        acts[par][r0:r0 + act_rows, sl] = (gelu_tanh(gate) * val).astype(BF16)

    def down_job(j, par, rb, blk, first):
        rows = slice(rb * down_rows, (rb + 1) * down_rows)
        cols = slice(blk * FF_CHUNK, (blk + 1) * FF_CHUNK)
        part = _dot(acts[par][rows], wdn_ref[j * FF_CHUNK:(j + 1) * FF_CHUNK, cols])
        if first:
            acc_ref[rows, cols] = part
        else:
            acc_ref[rows, cols] += part

    def step(up=None, act=None, down=None, first=False):
        u = [functools.partial(up_job, *up, half, rb)
             for half in range(2) for rb in range(FF_UP_JOBS)] if up else []
        a = [functools.partial(act_job, *act, rb, lh)
             for rb in range(FF_ACT_ROW_JOBS) for lh in range(FF_CHUNK // LANES)] if act else []
        d = [functools.partial(down_job, *down, rb, blk, first)
             for rb in range(FF_DOWN_ROW_JOBS) for blk in range(D_MODEL // FF_CHUNK)] if down else []
        n_col = D_MODEL // FF_CHUNK
        n_lh = FF_CHUNK // LANES
        for rb in range(FF_ACT_ROW_JOBS):
            for job in u[rb:rb + 1] + a[rb * n_lh:(rb + 1) * n_lh] + d[rb * n_col:(rb + 1) * n_col]:
                job()

    step(up=(0, 0))
    step(up=(1, 1), act=(0, 0), down=(0, 0), first=True)

    def pair(m, carry):
        j = 2 * m + 1
        step(up=(j + 1, 0), act=(j, 1), down=(j, 1))
        step(up=(j + 2, 1), act=(j + 1, 0), down=(j + 1, 0))
        return carry

    for m in range((n_chunks - 3) // 2):
        pair(m, 0)
    step(up=(n_chunks - 1, 0), act=(n_chunks - 2, 1), down=(n_chunks - 2, 1))
    step(act=(n_chunks - 1, 0), down=(n_chunks - 1, 0))
    o_ref[...] = x_ref[...] + _rms(acc_ref[...]) * gpost_ref[...]


def _ffn(x2, g_pre, w_up, conv_w, conv_b, w_down, g_post, seq):
    t = x2.shape[0]
    tm = TOKEN_TILE
    halo = FF_HALO
    tiles_per_seq = seq // tm
    n_chunks = D_FF // FF_CHUNK
    sub = tm // halo
    last = t // halo - 1
    return pl.pallas_call(
        functools.partial(_ffn_kernel, tm=tm, tiles_per_seq=tiles_per_seq, n_chunks=n_chunks),
        grid=(t // tm,),
        in_specs=[pl.BlockSpec((tm, D_MODEL), lambda i: (i, 0)),
                  pl.BlockSpec((halo, D_MODEL), lambda i: (jnp.maximum(i * sub - 1, 0), 0)),
                  pl.BlockSpec((halo, D_MODEL), lambda i: (jnp.minimum((i + 1) * sub, last), 0)),
                  _resident((1, D_MODEL)),
                  _resident((D_MODEL, 2 * D_FF)),
                  _resident((3, 2 * D_FF)),
                  _resident((1, 2 * D_FF)),
                  _resident((D_FF, D_MODEL)),
                  _resident((1, D_MODEL))],
        out_specs=pl.BlockSpec((tm, D_MODEL), lambda i: (i, 0)),
        out_shape=jax.ShapeDtypeStruct((t, D_MODEL), F32),
        scratch_shapes=[pltpu.VMEM((tm + 2 * halo, D_MODEL), BF16),
                        pltpu.VMEM((2, tm + 2 * halo, FF_CHUNK), F32),
                        pltpu.VMEM((2, tm + 2 * halo, FF_CHUNK), F32),
                        pltpu.VMEM((tm, FF_CHUNK), BF16),
                        pltpu.VMEM((tm, FF_CHUNK), BF16),
                        pltpu.VMEM((tm, D_MODEL), F32)],
        compiler_params=_params(1),
        name="conv_gated_mlp",
    )(x2, x2, x2, g_pre, w_up, conv_w, conv_b, w_down, g_post)


def _regroup_q_heads(w, axis):
    groups = A_HEADS // A_KV_HEADS
    shape = w.shape
    w = w.reshape(shape[:axis] + (A_KV_HEADS, groups, HEAD_DIM) + shape[axis + 1:])
    return jnp.swapaxes(w, axis, axis + 1).reshape(shape)


def _rope_tables(gain, scale, seq):
    t = jnp.arange(seq)
    half = HEAD_DIM // 2
    freqs = ROPE_THETA ** (-jnp.arange(0, half, 2, dtype=F32) / half)

    def axis_tables(pos):
        ang = pos.astype(F32)[:, None] * freqs[None, :]
        c, s = jnp.cos(ang), jnp.sin(ang)
        return jnp.concatenate([c, c], -1), jnp.concatenate([-s, s], -1)

    cr, sr = axis_tables(t // GRID_W)
    cc, sc = axis_tables(t % GRID_W)
    cos = jnp.concatenate([cr, cc], -1)
    sin = jnp.concatenate([sr, sc], -1)
    j = np.arange(HEAD_DIM)
    partner = np.where((j % half) < half // 2, j + half // 2, j - half // 2)
    g = gain.astype(F32)
    ctab = cos * (g * scale)[None, :]
    stab = sin * (g[partner] * scale)[None, :]
    return jnp.tile(ctab, (1, 2)), jnp.tile(stab, (1, 2))


def _natten_bias(rpb):
    c = np.arange(GRID_W)
    cs = np.clip(c - NA_COLS // 2, 0, GRID_W - NA_COLS)
    kc = np.arange(GRID_W)
    dc = kc[None, :] - c[:, None] + (NA_COLS - 1)
    valid = (kc[None, :] >= cs[:, None]) & (kc[None, :] < cs[:, None] + NA_COLS)
    row_t = jnp.where(valid[None, None], rpb.astype(F32)[:, :, np.clip(dc, 0, 2 * NA_COLS - 2)],
                      MASK_VALUE)
    n_e = 2 * NA_ROWS - 2
    tab = jnp.concatenate([row_t[:, :n_e], row_t[:, 1:n_e + 1]], axis=-1)
    tab = tab.reshape(C_HEADS // 2, 2, n_e, GRID_W, 2 * GRID_W).transpose(0, 2, 1, 3, 4)
    return tab.reshape(C_HEADS // 2, n_e, 2 * GRID_W, 2 * GRID_W)


def kernel(x, norm_mix_pre, norm_mix_post, norm_ffn_pre, norm_ffn_post, w_in, b_gate,
           qk_norm_q, qk_norm_k, w_pool, pool_scale, rpb, w_branch, w_out,
           w_up, conv_w, conv_b, w_down):
    b, seq, d = x.shape
    depth = w_in.shape[0]
    assert d == D_MODEL and seq % TOKEN_TILE == 0 and seq % ATTN_Q_TILE == 0
    assert seq // GRID_W >= NA_ROWS and seq % POOL_ROW_CHUNK == 0 and D_FF % FF_CHUNK == 0
    n_chunks = D_FF // FF_CHUNK
    assert n_chunks % 2 == 1 and n_chunks >= 3
    hh = np.arange(2 * LANES) % LANES // HEAD_DIM
    ones_bd = jnp.asarray((hh[:, None] == np.arange(LANES)[None, :] // HEAD_DIM) / HEAD_DIM, BF16)
    row2 = lambda v: v.astype(F32).reshape(1, -1)

    x2 = x.reshape(b * seq, d)
    for l in range(depth):
        w_l = w_in[l]
        w_qkv = jnp.concatenate([_regroup_q_heads(w_l[:, :512], 1), w_l[:, 512:W_QKV]],
                                axis=1).astype(BF16)
        w_gate = w_l[:, W_QKV:].astype(BF16)
        cq, sq = _rope_tables(qk_norm_q[l], QK_SCALE, seq)
        ck, sk = _rope_tables(qk_norm_k[l], 1.0, seq)
        wb = w_branch[l]
        wb = jnp.concatenate([_regroup_q_heads(wb[0], 0)[None], wb[1:]], axis=0).astype(BF16)
        wup = w_up[l].astype(BF16)
        cw = conv_w[l].astype(F32)
        cb = row2(conv_b[l])
        wdn = w_down[l].astype(BF16)

        g_pre = row2(norm_mix_pre[l])
        qa, ka, va, pu, qc, kc, vc = _inproj(x2, g_pre, w_qkv, cq, sq, ck, sk, ones_bd, seq)
        ya = _global_attention(qa, ka, va, seq)
        yb = _pool_mixer(pu, w_pool[l].astype(BF16), row2(pool_scale[l]), seq)
        yc = _neighbourhood_attention(qc, kc, vc, _natten_bias(rpb[l]), seq)
        x2 = _merge(x2, ya, yb, yc, g_pre, w_gate, row2(b_gate[l]), wb, w_out[l].astype(BF16),
                    row2(norm_mix_post[l]))
        x2 = _ffn(x2, row2(norm_ffn_pre[l]), wup, cw, cb, wdn, row2(norm_ffn_post[l]), seq)
    return x2.reshape(b, seq, d)
```

```python
import functools

import numpy as np
import jax
import jax.numpy as jnp
from jax import lax
from jax.experimental import pallas as pl
from jax.experimental.pallas import tpu as pltpu

F32 = jnp.float32
BF16 = jnp.bfloat16

D_MODEL = 1024
GRID_W = 64
HEAD_DIM = 64
A_HEADS = 8
A_KV_HEADS = 2
ROPE_THETA = 10000.0
POOL_WINDOWS = (2, 4, 8, 16)
POOL_GROUP = 128
C_HEADS = 8
NA_ROWS = 8
NA_COLS = 16
N_BRANCH = 3
BRANCH_WIDTH = 512
D_FF = 2816
EPS = 1e-6
QK_SCALE = HEAD_DIM ** -0.5
GELU_C = float(np.sqrt(2.0 / np.pi))
GELU_A = 0.044715
LOG2_E = float(np.log2(np.e))

LANES = 128
V7X_VMEM_BYTES = 64 * 1024 * 1024
VMEM_LIMIT = 56 * 1024 * 1024

TOKEN_TILE = 512
ATTN_Q_TILE = 256
POOL_ROW_CHUNK = 256
POOL_HALO = 16
FF_CHUNK = 256
FF_HALO = 16
FF_UP_JOBS = 2
FF_ACT_ROW_JOBS = 4
FF_DOWN_ROW_JOBS = 4
MASK_VALUE = -1e30
NA_ROW_UNROLL = 2

OFF_QA = 0
OFF_KVA = 512
OFF_POOL = 768
OFF_QC = 1280
OFF_KC = 1792
OFF_VC = 2304
W_QKV = 2816


def _params(n_grid, flags=None):
    return pltpu.CompilerParams(
        dimension_semantics=("arbitrary",) * n_grid, vmem_limit_bytes=VMEM_LIMIT, flags=flags)


def _resident(shape):
    zeros = (0,) * len(shape)
    return pl.BlockSpec(shape, lambda *_: zeros, pipeline_mode=pl.Buffered(1))


def _rms(x):
    return x * lax.rsqrt(jnp.mean(x * x, axis=-1, keepdims=True) + EPS)


def _dot(a, b):
    return jnp.dot(a, b, preferred_element_type=F32)


def _dot_nt(a, b):
    return lax.dot_general(a, b, (((1,), (1,)), ((), ())), preferred_element_type=F32)


def _inproj_kernel(x_ref, g_ref, w_ref, cq_ref, sq_ref, ck_ref, sk_ref, ones_ref,
                   qa_ref, ka_ref, va_ref, pu_ref, qc_ref, kc_ref, vc_ref,
                   *, tm, tiles_per_seq):
    i = pl.program_id(0)
    h = (_rms(x_ref[...]) * g_ref[...]).astype(BF16)
    row0 = pl.multiple_of((i % tiles_per_seq) * tm, tm)
    lane = lax.broadcasted_iota(jnp.int32, (tm, LANES), 1)
    first_half = (lane % 32) < 16

    def norm_rope(u, c_ref, s_ref):
        u2 = u * u
        hi = u2.astype(BF16)
        lo = (u2 - hi.astype(F32)).astype(BF16)
        ms = _dot(jnp.concatenate([hi, lo], axis=1), ones_ref[...])
        partner = jnp.where(first_half, pltpu.roll(u, LANES - 16, 1), pltpu.roll(u, 16, 1))
        c = c_ref[pl.ds(row0, tm), :]
        s = s_ref[pl.ds(row0, tm), :]
        return (u * c + partner * s) * lax.rsqrt(ms + EPS)

    def q_chunk(c):
        sl = slice(c * LANES, (c + 1) * LANES)
        qa_ref[:, sl] = norm_rope(qa[:, sl], cq_ref, sq_ref).astype(BF16)

    qa = _dot(h, w_ref[:, OFF_QA:OFF_QA + 512])
    kv = _dot(h, w_ref[:, OFF_KVA:OFF_KVA + 256])
    q_chunk(0)
    pu_ref[...] = _dot(h, w_ref[:, OFF_POOL:OFF_POOL + 512])
    q_chunk(1)
    qc_ref[...] = (_dot(h, w_ref[:, OFF_QC:OFF_QC + 512]) * QK_SCALE).astype(BF16)
    q_chunk(2)
    kc_ref[...] = _dot(h, w_ref[:, OFF_KC:OFF_KC + 512]).astype(BF16)
    q_chunk(3)
    vc_ref[...] = _dot(h, w_ref[:, OFF_VC:OFF_VC + 512]).astype(BF16)
    ka_ref[...] = norm_rope(kv[:, :LANES], ck_ref, sk_ref).astype(BF16)
    va_ref[...] = kv[:, LANES:].astype(BF16)


def _inproj(x2, g_pre, w_qkv, cq, sq, ck, sk, ones_bd, seq):
    t = x2.shape[0]
    tm = TOKEN_TILE
    tiles_per_seq = seq // tm
    row = lambda width: pl.BlockSpec((tm, width), lambda i: (i, 0))
    out_widths = (512, LANES, LANES, 512, 512, 512, 512)
    out_dtypes = (BF16, BF16, BF16, F32, BF16, BF16, BF16)
    out_specs = [row(w) for w in out_widths]
    out_shape = [jax.ShapeDtypeStruct((t, w), d) for w, d in zip(out_widths, out_dtypes)]
    return pl.pallas_call(
        functools.partial(_inproj_kernel, tm=tm, tiles_per_seq=tiles_per_seq),
        grid=(t // tm,),
        in_specs=[row(D_MODEL), _resident((1, D_MODEL)), _resident((D_MODEL, W_QKV)),
                  _resident((seq, LANES)), _resident((seq, LANES)),
                  _resident((seq, LANES)), _resident((seq, LANES)),
                  _resident((2 * LANES, LANES))],
        out_specs=out_specs,
        out_shape=out_shape,
        compiler_params=_params(1),
        name="inproj",
    )(x2, g_pre, w_qkv, cq, sq, ck, sk, ones_bd)


def _split_stack(x, lo):
    zero = jnp.zeros_like(x)
    return jnp.concatenate([jnp.where(lo, x, zero), jnp.where(lo, zero, x)], axis=0)


def _softmax_pv(s, v):
    m = jnp.max(s, axis=-1, keepdims=True)
    p = jnp.exp(s - m)
    l = jnp.sum(p, axis=-1, keepdims=True)
    return _dot(p.astype(BF16), v) / l


def _attn_kernel(q_ref, k_ref, v_ref, o_ref, s_a, s_b, p_a, p_b, *, tq):
    n_chunks = 4
    seq = k_ref.shape[0]
    lo = lax.broadcasted_iota(jnp.int32, (tq, LANES), 1) < HEAD_DIM
    ones = jnp.ones((seq, LANES), BF16)
    lanes = lambda c: slice(c * LANES, (c + 1) * LANES)
    s_bufs = (s_a, s_b)
    p_bufs = (p_a, p_b)

    def score(c):
        s_bufs[c % 2][...] = _dot_nt(_split_stack(q_ref[:, lanes(c)], lo), k_ref[...])

    def softmax(c):
        s = s_bufs[c % 2][...]
        p_bufs[c % 2][...] = jnp.exp(s - jnp.max(s, axis=-1, keepdims=True)).astype(BF16)

    def value(c):
        v_aug = jnp.concatenate([v_ref[...], ones], axis=1)
        pv = _dot(p_bufs[c % 2][...], v_aug)
        pv = pv[:, :LANES] / pv[:, LANES:]
        o_ref[:, lanes(c)] = jnp.where(lo, pv[:tq], pv[tq:]).astype(o_ref.dtype)

    for t in range(n_chunks + 2):
        if t < n_chunks:
            score(t)
        if 0 <= t - 1 < n_chunks:
            softmax(t - 1)
        if 0 <= t - 2 < n_chunks:
            value(t - 2)


def _global_attention(qa, ka, va, seq):
    t = qa.shape[0]
    tq = ATTN_Q_TILE
    nq = seq // tq
    return pl.pallas_call(
        functools.partial(_attn_kernel, tq=tq),
        grid=(t // seq, nq),
        in_specs=[pl.BlockSpec((tq, 512), lambda b, j: (b * nq + j, 0)),
                  pl.BlockSpec((seq, LANES), lambda b, j: (b, 0)),
                  pl.BlockSpec((seq, LANES), lambda b, j: (b, 0))],
        out_specs=pl.BlockSpec((tq, 512), lambda b, j: (b * nq + j, 0)),
        out_shape=jax.ShapeDtypeStruct((t, 512), BF16),
        scratch_shapes=[pltpu.VMEM((2 * tq, seq), F32), pltpu.VMEM((2 * tq, seq), F32),
                        pltpu.VMEM((2 * tq, seq), BF16), pltpu.VMEM((2 * tq, seq), BF16)],
        compiler_params=_params(2),
        name="global_attention",
    )(qa, ka, va)


def _pool_kernel(u_ref, w_ref, scale_ref, o_ref, pad_ref, *, seq):
    width = u_ref.shape[1]
    zeros = jnp.zeros((POOL_HALO, width), F32)
    pad_ref[0:POOL_HALO, :] = zeros
    pad_ref[POOL_HALO + seq:, :] = zeros
    pad_ref[POOL_HALO:POOL_HALO + seq, :] = u_ref[...]
    rc = POOL_ROW_CHUNK
    margin = 8
    n_ext = rc + 2 * margin
    assert max(POOL_WINDOWS) // 2 <= margin <= POOL_HALO
    back = lambda y, k: pltpu.roll(y, k, 0)
    ahead = lambda y, k: pltpu.roll(y, n_ext - k, 0)
    for r0 in range(0, seq, rc):
        t = r0 + lax.broadcasted_iota(jnp.int32, (rc, POOL_GROUP), 0)
        for g, w in enumerate(POOL_WINDOWS):
            sl = slice(g * POOL_GROUP, (g + 1) * POOL_GROUP)
            half = w // 2
            base = POOL_HALO + r0 - margin
            run = pad_ref[base:base + n_ext, sl]
            width = 1
            while width < half:
                run = run + ahead(run, width)
                width *= 2
            acc = (back(run, half) + run)[margin:margin + rc]
            cnt = jnp.minimum(t + (w - half), seq) - jnp.maximum(t - half, 0)
            mean = acc / cnt.astype(F32)
            diff = (mean - u_ref[r0:r0 + rc, sl]).astype(BF16)
            y = _dot(diff, w_ref[g]) * scale_ref[:, sl]
            o_ref[r0:r0 + rc, sl] = y.astype(o_ref.dtype)


def _pool_mixer(pu, w_pool, pool_scale, seq):
    t, width = pu.shape
    n_g = len(POOL_WINDOWS)
    return pl.pallas_call(
        functools.partial(_pool_kernel, seq=seq),
        grid=(t // seq,),
        in_specs=[pl.BlockSpec((seq, width), lambda b: (b, 0)),
                  _resident((n_g, POOL_GROUP, POOL_GROUP)),
                  _resident((1, width))],
        out_specs=pl.BlockSpec((seq, width), lambda b: (b, 0)),
        out_shape=jax.ShapeDtypeStruct((t, width), BF16),
        scratch_shapes=[pltpu.VMEM((seq + 2 * POOL_HALO, width), F32)],
        compiler_params=_params(1),
        name="pool_mixer",
    )(pu, w_pool, pool_scale)


def _natten_kernel(q_ref, k_ref, v_ref, bias_ref, o_ref, s_a, s_b, p_a, p_b, *, rows):
    win = NA_ROWS * GRID_W
    n_pairs = C_HEADS // 2
    n_steps = rows // NA_ROW_UNROLL
    lo = lax.broadcasted_iota(jnp.int32, (GRID_W, LANES), 1) < HEAD_DIM
    ones = jnp.ones((win, LANES), BF16)
    lanes = lambda pr: slice(pr * LANES, (pr + 1) * LANES)
    s_bufs = (s_a, s_b)
    p_bufs = (p_a, p_b)

    def jobs_of(step):
        jobs = []
        for u in range(NA_ROW_UNROLL):
            r = step * NA_ROW_UNROLL + u
            rs = min(max(r - NA_ROWS // 2, 0), rows - NA_ROWS)
            kind = rs - r + (NA_ROWS - 1)
            q0 = r * GRID_W
            k0 = rs * GRID_W
            jobs.extend((u * n_pairs + pr, pr, kind, q0, k0) for pr in range(n_pairs))
        return jobs

    def score_job(par, n, pr, kind, q0, k0):
        qs = _split_stack(q_ref[pl.ds(q0, GRID_W), lanes(pr)], lo)
        bias = jnp.concatenate([bias_ref[pr, kind + 2 * jj] for jj in range(NA_ROWS // 2)], axis=1)
        s_bufs[par][n] = _dot_nt(qs, k_ref[pl.ds(k0, win), lanes(pr)]) + bias

    def softmax_job(par, n, pr, kind, q0, k0):
        s = s_bufs[par][n]
        p_bufs[par][n] = jnp.exp(s - jnp.max(s, axis=-1, keepdims=True)).astype(BF16)

    def value_job(par, n, pr, kind, q0, k0):
        v_aug = jnp.concatenate([v_ref[pl.ds(k0, win), lanes(pr)], ones], axis=1)
        pv = _dot(p_bufs[par][n], v_aug)
        pv = pv[:, :LANES] / pv[:, LANES:]
        o_ref[pl.ds(q0, GRID_W), lanes(pr)] = jnp.where(
            lo, pv[:GRID_W], pv[GRID_W:]).astype(o_ref.dtype)

    def emit(score_step=None, softmax_step=None, value_step=None):
        plan = []
        for step, par, fn in (score_step, softmax_step, value_step):
            plan.append([(fn, par, job) for job in jobs_of(step)] if step is not None else [])
        for n in range(NA_ROW_UNROLL * n_pairs):
            for stage in plan:
                if stage:
                    fn, par, job = stage[n]
                    fn(par, *job)

    sc = lambda step, par: (step, par, score_job)
    sm = lambda step, par: (step, par, softmax_job)
    va = lambda step, par: (step, par, value_job)
    none = (None, None, None)

    emit(sc(0, 0), none, none)
    emit(sc(1, 1), sm(0, 0), none)

    def body(m, carry):
        i = 2 * m + 1
        emit(sc(i + 1, 0), sm(i, 1), va(i - 1, 0))
        emit(sc(i + 2, 1), sm(i + 1, 0), va(i, 1))
        return carry

    for m in range((n_steps - 2) // 2):
        body(m, 0)
    emit(none, sm(n_steps - 1, 1), va(n_steps - 2, 0))
    emit(none, none, va(n_steps - 1, 1))


def _neighbourhood_attention(qc, kc, vc, bias, seq):
    t, width = qc.shape
    rows = seq // GRID_W
    n_steps = rows // NA_ROW_UNROLL
    assert rows % NA_ROW_UNROLL == 0 and n_steps % 2 == 0 and n_steps >= 4
    n_jobs = NA_ROW_UNROLL * C_HEADS // 2
    blk = pl.BlockSpec((seq, width), lambda b: (b, 0))
    s_buf = pltpu.VMEM((n_jobs, 2 * GRID_W, NA_ROWS * GRID_W), F32)
    p_buf = pltpu.VMEM((n_jobs, 2 * GRID_W, NA_ROWS * GRID_W), BF16)
    return pl.pallas_call(
        functools.partial(_natten_kernel, rows=rows),
        grid=(t // seq,),
        in_specs=[blk, blk, blk, _resident(bias.shape)],
        out_specs=blk,
        out_shape=jax.ShapeDtypeStruct((t, width), BF16),
        scratch_shapes=[s_buf, s_buf, p_buf, p_buf],
        compiler_params=_params(1),
        name="neighbourhood_attention",
    )(qc, kc, vc, bias)


def _merge_kernel(x_ref, ya_ref, yb_ref, yc_ref, gpre_ref, wg_ref, bg_ref, wb_ref, wo_ref,
                  gpost_ref, o_ref):
    x = x_ref[...]
    h = (_rms(x) * gpre_ref[...]).astype(BF16)
    merged = None
    for n, y_ref in enumerate((ya_ref, yb_ref, yc_ref)):
        sl = slice(n * D_MODEL, (n + 1) * D_MODEL)
        gate = jax.nn.sigmoid(_dot(h, wg_ref[:, sl]) + bg_ref[:, sl])
        term = gate * _dot(y_ref[...], wb_ref[n])
        merged = term if merged is None else merged + term
    mix = _dot(merged.astype(BF16), wo_ref[...])
    o_ref[...] = x + _rms(mix) * gpost_ref[...]


def _merge(x2, ya, yb, yc, g_pre, w_gate, b_gate, w_branch, w_out, g_post):
    t = x2.shape[0]
    tm = TOKEN_TILE
    row = lambda width: pl.BlockSpec((tm, width), lambda i: (i, 0))
    return pl.pallas_call(
        _merge_kernel,
        grid=(t // tm,),
        in_specs=[row(D_MODEL), row(BRANCH_WIDTH), row(BRANCH_WIDTH), row(BRANCH_WIDTH),
                  _resident((1, D_MODEL)), _resident((D_MODEL, N_BRANCH * D_MODEL)),
                  _resident((1, N_BRANCH * D_MODEL)),
                  _resident((N_BRANCH, BRANCH_WIDTH, D_MODEL)),
                  _resident((D_MODEL, D_MODEL)), _resident((1, D_MODEL))],
        out_specs=row(D_MODEL),
        out_shape=jax.ShapeDtypeStruct((t, D_MODEL), F32),
        compiler_params=_params(1),
        name="merge",
    )(x2, ya, yb, yc, g_pre, w_gate, b_gate, w_branch, w_out, g_post)


def _ffn_kernel(x_ref, xp_ref, xn_ref, gpre_ref, wup_ref, cw_ref, cb_ref, wdn_ref, gpost_ref,
                o_ref, h_ref, up_a, up_b, act_a, act_b, acc_ref, *, tm, tiles_per_seq, n_chunks):
    i = pl.program_id(0)
    pos = i % tiles_per_seq
    halo = FF_HALO
    has_prev = (pos > 0).astype(F32)
    has_next = (pos < tiles_per_seq - 1).astype(F32)
    xs = jnp.concatenate([xp_ref[...] * has_prev, x_ref[...], xn_ref[...] * has_next], axis=0)
    h_ref[...] = (_rms(xs) * gpre_ref[...]).astype(BF16)
    ups = (up_a, up_b)
    acts = (act_a, act_b)

    up_rows = (tm + 2 * halo) // FF_UP_JOBS
    act_rows = tm // FF_ACT_ROW_JOBS
    down_rows = tm // FF_DOWN_ROW_JOBS

    def hidden(half, j, start, width):
        off = half * D_FF + j * FF_CHUNK + start
        return slice(off, off + width)

    def up_job(j, par, half, rb):
        rows = slice(rb * up_rows, (rb + 1) * up_rows)
        ups[par][half, rows] = _dot(h_ref[rows], wup_ref[:, hidden(half, j, 0, FF_CHUNK)])

    def conv(up_ref, half, j, r0, sl):
        top = halo + r0 - 8
        n_ext = act_rows + 16
        ext = up_ref[half, top:top + n_ext, sl]
        inner = lambda y: y[8:8 + act_rows]
        before = inner(pltpu.roll(ext, 1, 0))
        after = inner(pltpu.roll(ext, n_ext - 1, 0))
        cols = hidden(half, j, sl.start, LANES)
        w = lambda k: cw_ref[k:k + 1, cols]
        return before * w(0) + inner(ext) * w(1) + after * w(2) + cb_ref[:, cols]

    def gelu_tanh(g):
        k1 = -2.0 * GELU_C * LOG2_E
        return g / (1.0 + jnp.exp2(g * (k1 + (k1 * GELU_A) * (g * g))))

    def act_job(j, par, rb, lane_half):
        r0 = rb * act_rows
        sl = slice(lane_half * LANES, (lane_half + 1) * LANES)
        val = conv(ups[par], 0, j, r0, sl)
        gate = conv(ups[par], 1, j, r0, sl)
        acts[par][r0:r0 + act_rows, sl] = (gelu_tanh(gate) * val).astype(BF16)

    def down_job(j, par, rb, blk, first):
        rows = slice(rb * down_rows, (rb + 1) * down_rows)
        cols = slice(blk * FF_CHUNK, (blk + 1) * FF_CHUNK)
        part = _dot(acts[par][rows], wdn_ref[j * FF_CHUNK:(j + 1) * FF_CHUNK, cols])
        if first:
            acc_ref[rows, cols] = part
        else:
            acc_ref[rows, cols] += part

    def step(up=None, act=None, down=None, first=False):
        u = [functools.partial(up_job, *up, half, rb)
             for half in range(2) for rb in range(FF_UP_JOBS)] if up else []
        a = [functools.partial(act_job, *act, rb, lh)
             for rb in range(FF_ACT_ROW_JOBS) for lh in range(FF_CHUNK // LANES)] if act else []
        d = [functools.partial(down_job, *down, rb, blk, first)
             for rb in range(FF_DOWN_ROW_JOBS) for blk in range(D_MODEL // FF_CHUNK)] if down else []
        n_col = D_MODEL // FF_CHUNK
        n_lh = FF_CHUNK // LANES
        for rb in range(FF_ACT_ROW_JOBS):
            for job in u[rb:rb + 1] + a[rb * n_lh:(rb + 1) * n_lh] + d[rb * n_col:(rb + 1) * n_col]:
                job()

    step(up=(0, 0))
    step(up=(1, 1), act=(0, 0), down=(0, 0), first=True)

    def pair(m, carry):
        j = 2 * m + 1
        step(up=(j + 1, 0), act=(j, 1), down=(j, 1))
        step(up=(j + 2, 1), act=(j + 1, 0), down=(j + 1, 0))
        return carry

    for m in range((n_chunks - 3) // 2):
        pair(m, 0)
    step(up=(n_chunks - 1, 0), act=(n_chunks - 2, 1), down=(n_chunks - 2, 1))
    step(act=(n_chunks - 1, 0), down=(n_chunks - 1, 0))
    o_ref[...] = x_ref[...] + _rms(acc_ref[...]) * gpost_ref[...]


def _ffn(x2, g_pre, w_up, conv_w, conv_b, w_down, g_post, seq):
    t = x2.shape[0]
    tm = TOKEN_TILE
    halo = FF_HALO
    tiles_per_seq = seq // tm
    n_chunks = D_FF // FF_CHUNK
    sub = tm // halo
    last = t // halo - 1
    return pl.pallas_call(
        functools.partial(_ffn_kernel, tm=tm, tiles_per_seq=tiles_per_seq, n_chunks=n_chunks),
        grid=(t // tm,),
        in_specs=[pl.BlockSpec((tm, D_MODEL), lambda i: (i, 0)),
                  pl.BlockSpec((halo, D_MODEL), lambda i: (jnp.maximum(i * sub - 1, 0), 0)),
                  pl.BlockSpec((halo, D_MODEL), lambda i: (jnp.minimum((i + 1) * sub, last), 0)),
                  _resident((1, D_MODEL)),
                  _resident((D_MODEL, 2 * D_FF)),
                  _resident((3, 2 * D_FF)),
                  _resident((1, 2 * D_FF)),
                  _resident((D_FF, D_MODEL)),
                  _resident((1, D_MODEL))],
        out_specs=pl.BlockSpec((tm, D_MODEL), lambda i: (i, 0)),
        out_shape=jax.ShapeDtypeStruct((t, D_MODEL), F32),
        scratch_shapes=[pltpu.VMEM((tm + 2 * halo, D_MODEL), BF16),
                        pltpu.VMEM((2, tm + 2 * halo, FF_CHUNK), F32),
                        pltpu.VMEM((2, tm + 2 * halo, FF_CHUNK), F32),
                        pltpu.VMEM((tm, FF_CHUNK), BF16),
                        pltpu.VMEM((tm, FF_CHUNK), BF16),
                        pltpu.VMEM((tm, D_MODEL), F32)],
        compiler_params=_params(1),
        name="conv_gated_mlp",
    )(x2, x2, x2, g_pre, w_up, conv_w, conv_b, w_down, g_post)


def _regroup_q_heads(w, axis):
    groups = A_HEADS // A_KV_HEADS
    shape = w.shape
    w = w.reshape(shape[:axis] + (A_KV_HEADS, groups, HEAD_DIM) + shape[axis + 1:])
    return jnp.swapaxes(w, axis, axis + 1).reshape(shape)


def _rope_tables(gain, scale, seq):
    t = jnp.arange(seq)
    half = HEAD_DIM // 2
    freqs = ROPE_THETA ** (-jnp.arange(0, half, 2, dtype=F32) / half)

    def axis_tables(pos):
        ang = pos.astype(F32)[:, None] * freqs[None, :]
        c, s = jnp.cos(ang), jnp.sin(ang)
        return jnp.concatenate([c, c], -1), jnp.concatenate([-s, s], -1)

    cr, sr = axis_tables(t // GRID_W)
    cc, sc = axis_tables(t % GRID_W)
    cos = jnp.concatenate([cr, cc], -1)
    sin = jnp.concatenate([sr, sc], -1)
    j = np.arange(HEAD_DIM)
    partner = np.where((j % half) < half // 2, j + half // 2, j - half // 2)
    g = gain.astype(F32)
    ctab = cos * (g * scale)[None, :]
    stab = sin * (g[partner] * scale)[None, :]
    return jnp.tile(ctab, (1, 2)), jnp.tile(stab, (1, 2))


def _natten_bias(rpb):
    c = np.arange(GRID_W)
    cs = np.clip(c - NA_COLS // 2, 0, GRID_W - NA_COLS)
    kc = np.arange(GRID_W)
    valid = (kc[None, :] >= cs[:, None]) & (kc[None, :] < cs[:, None] + NA_COLS)
    pad = GRID_W - NA_COLS
    padded = jnp.pad(rpb.astype(F32), ((0, 0), (0, 0), (pad, pad)))
    row_t = jnp.stack([padded[:, :, GRID_W - 1 - ci:2 * GRID_W - 1 - ci] for ci in range(GRID_W)], axis=2)
    row_t = jnp.where(valid[None, None], row_t, MASK_VALUE)
    n_e = 2 * NA_ROWS - 2
    tab = jnp.concatenate([row_t[:, :n_e], row_t[:, 1:n_e + 1]], axis=-1)
    tab = tab.reshape(C_HEADS // 2, 2, n_e, GRID_W, 2 * GRID_W).transpose(0, 2, 1, 3, 4)
    return tab.reshape(C_HEADS // 2, n_e, 2 * GRID_W, 2 * GRID_W)


def kernel(x, norm_mix_pre, norm_mix_post, norm_ffn_pre, norm_ffn_post, w_in, b_gate,
           qk_norm_q, qk_norm_k, w_pool, pool_scale, rpb, w_branch, w_out,
           w_up, conv_w, conv_b, w_down):
    b, seq, d = x.shape
    depth = w_in.shape[0]
    assert d == D_MODEL and seq % TOKEN_TILE == 0 and seq % ATTN_Q_TILE == 0
    assert seq // GRID_W >= NA_ROWS and seq % POOL_ROW_CHUNK == 0 and D_FF % FF_CHUNK == 0
    n_chunks = D_FF // FF_CHUNK
    assert n_chunks % 2 == 1 and n_chunks >= 3
    hh = np.arange(2 * LANES) % LANES // HEAD_DIM
    ones_bd = jnp.asarray((hh[:, None] == np.arange(LANES)[None, :] // HEAD_DIM) / HEAD_DIM, BF16)
    row2 = lambda v: v.astype(F32).reshape(1, -1)

    x2 = x.reshape(b * seq, d)
    for l in range(depth):
        w_l = w_in[l]
        w_qkv = jnp.concatenate([_regroup_q_heads(w_l[:, :512], 1), w_l[:, 512:W_QKV]],
                                axis=1).astype(BF16)
        w_gate = w_l[:, W_QKV:].astype(BF16)
        cq, sq = _rope_tables(qk_norm_q[l], QK_SCALE, seq)
        ck, sk = _rope_tables(qk_norm_k[l], 1.0, seq)
        wb = w_branch[l]
        wb = jnp.concatenate([_regroup_q_heads(wb[0], 0)[None], wb[1:]], axis=0).astype(BF16)
        wup = w_up[l].astype(BF16)
        cw = conv_w[l].astype(F32)
        cb = row2(conv_b[l])
        wdn = w_down[l].astype(BF16)

        g_pre = row2(norm_mix_pre[l])
        qa, ka, va, pu, qc, kc, vc = _inproj(x2, g_pre, w_qkv, cq, sq, ck, sk, ones_bd, seq)
        ya = _global_attention(qa, ka, va, seq)
        yb = _pool_mixer(pu, w_pool[l].astype(BF16), row2(pool_scale[l]), seq)
        yc = _neighbourhood_attention(qc, kc, vc, _natten_bias(rpb[l]), seq)
        x2 = _merge(x2, ya, yb, yc, g_pre, w_gate, row2(b_gate[l]), wb, w_out[l].astype(BF16),
                    row2(norm_mix_post[l]))
        x2 = _ffn(x2, row2(norm_ffn_pre[l]), wup, cw, cb, wdn, row2(norm_ffn_post[l]), seq)
    return x2.reshape(b, seq, d)
```

```python
import functools

import numpy as np
import jax
import jax.numpy as jnp
from jax import lax
from jax.experimental import pallas as pl
from jax.experimental.pallas import tpu as pltpu

F32 = jnp.float32
BF16 = jnp.bfloat16

D_MODEL = 1024
GRID_W = 64
HEAD_DIM = 64
A_HEADS = 8
A_KV_HEADS = 2
ROPE_THETA = 10000.0
POOL_WINDOWS = (2, 4, 8, 16)
POOL_GROUP = 128
C_HEADS = 8
NA_ROWS = 8
NA_COLS = 16
N_BRANCH = 3
BRANCH_WIDTH = 512
D_FF = 2816
EPS = 1e-6
QK_SCALE = HEAD_DIM ** -0.5
GELU_C = float(np.sqrt(2.0 / np.pi))
GELU_A = 0.044715
LOG2_E = float(np.log2(np.e))

LANES = 128
V7X_VMEM_BYTES = 64 * 1024 * 1024
VMEM_LIMIT = 56 * 1024 * 1024

TOKEN_TILE = 1024
ATTN_Q_TILE = 512
POOL_ROW_CHUNK = 256
POOL_HALO = 16
FF_TOKEN_TILE = 512
FF_CHUNK = 256
FF_HALO = 16
FF_UP_JOBS = 2
FF_ACT_ROW_JOBS = 4
FF_DOWN_ROW_JOBS = 4
MASK_VALUE = -1e30
NA_ROW_UNROLL = 2

OFF_QA = 0
OFF_KVA = 512
OFF_POOL = 768
OFF_QC = 1280
OFF_KC = 1792
OFF_VC = 2304
W_QKV = 2816


def _params(n_grid, flags=None):
    return pltpu.CompilerParams(
        dimension_semantics=("arbitrary",) * n_grid, vmem_limit_bytes=VMEM_LIMIT, flags=flags)


def _resident(shape):
    zeros = (0,) * len(shape)
    return pl.BlockSpec(shape, lambda *_: zeros, pipeline_mode=pl.Buffered(1))


def _rms(x):
    return x * lax.rsqrt(jnp.mean(x * x, axis=-1, keepdims=True) + EPS)


def _dot(a, b):
    return jnp.dot(a, b, preferred_element_type=F32)


def _dot_nt(a, b):
    return lax.dot_general(a, b, (((1,), (1,)), ((), ())), preferred_element_type=F32)


def _inproj_kernel(x_ref, g_ref, w_ref, cq_ref, sq_ref, ck_ref, sk_ref, ones_ref,
                   qa_ref, ka_ref, va_ref, pu_ref, qc_ref, kc_ref, vc_ref,
                   *, tm, tiles_per_seq):
    i = pl.program_id(0)
    h = (_rms(x_ref[...]) * g_ref[...]).astype(BF16)
    row0 = pl.multiple_of((i % tiles_per_seq) * tm, tm)
    lane = lax.broadcasted_iota(jnp.int32, (tm, LANES), 1)
    first_half = (lane % 32) < 16

    def norm_rope(u, c_ref, s_ref):
        u2 = u * u
        hi = u2.astype(BF16)
        lo = (u2 - hi.astype(F32)).astype(BF16)
        ms = _dot(jnp.concatenate([hi, lo], axis=1), ones_ref[...])
        partner = jnp.where(first_half, pltpu.roll(u, LANES - 16, 1), pltpu.roll(u, 16, 1))
        c = c_ref[pl.ds(row0, tm), :]
        s = s_ref[pl.ds(row0, tm), :]
        return (u * c + partner * s) * lax.rsqrt(ms + EPS)

    def q_chunk(c):
        sl = slice(c * LANES, (c + 1) * LANES)
        qa_ref[:, sl] = norm_rope(qa[:, sl], cq_ref, sq_ref).astype(BF16)

    qa = _dot(h, w_ref[:, OFF_QA:OFF_QA + 512])
    kv = _dot(h, w_ref[:, OFF_KVA:OFF_KVA + 256])
    q_chunk(0)
    pu_ref[...] = _dot(h, w_ref[:, OFF_POOL:OFF_POOL + 512])
    q_chunk(1)
    qc_ref[...] = (_dot(h, w_ref[:, OFF_QC:OFF_QC + 512]) * QK_SCALE).astype(BF16)
    q_chunk(2)
    kc_ref[...] = _dot(h, w_ref[:, OFF_KC:OFF_KC + 512]).astype(BF16)
    q_chunk(3)
    vc_ref[...] = _dot(h, w_ref[:, OFF_VC:OFF_VC + 512]).astype(BF16)
    ka_ref[...] = norm_rope(kv[:, :LANES], ck_ref, sk_ref).astype(BF16)
    va_ref[...] = kv[:, LANES:].astype(BF16)


def _inproj(x2, g_pre, w_qkv, cq, sq, ck, sk, ones_bd, seq):
    t = x2.shape[0]
    tm = TOKEN_TILE
    tiles_per_seq = seq // tm
    row = lambda width: pl.BlockSpec((tm, width), lambda i: (i, 0))
    out_widths = (512, LANES, LANES, 512, 512, 512, 512)
    out_dtypes = (BF16, BF16, BF16, F32, BF16, BF16, BF16)
    out_specs = [row(w) for w in out_widths]
    out_shape = [jax.ShapeDtypeStruct((t, w), d) for w, d in zip(out_widths, out_dtypes)]
    return pl.pallas_call(
        functools.partial(_inproj_kernel, tm=tm, tiles_per_seq=tiles_per_seq),
        grid=(t // tm,),
        in_specs=[row(D_MODEL), _resident((1, D_MODEL)), _resident((D_MODEL, W_QKV)),
                  _resident((seq, LANES)), _resident((seq, LANES)),
                  _resident((seq, LANES)), _resident((seq, LANES)),
                  _resident((2 * LANES, LANES))],
        out_specs=out_specs,
        out_shape=out_shape,
        compiler_params=_params(1),
        name="inproj",
    )(x2, g_pre, w_qkv, cq, sq, ck, sk, ones_bd)


def _split_stack(x, lo):
    zero = jnp.zeros_like(x)
    return jnp.concatenate([jnp.where(lo, x, zero), jnp.where(lo, zero, x)], axis=0)


def _softmax_pv(s, v):
    m = jnp.max(s, axis=-1, keepdims=True)
    p = jnp.exp(s - m)
    l = jnp.sum(p, axis=-1, keepdims=True)
    return _dot(p.astype(BF16), v) / l


def _attn_kernel(q_ref, k_ref, v_ref, o_ref, s_a, s_b, p_a, p_b, *, tq):
    n_chunks = 4
    seq = k_ref.shape[0]
    lo = lax.broadcasted_iota(jnp.int32, (tq, LANES), 1) < HEAD_DIM
    ones = jnp.ones((seq, LANES), BF16)
    lanes = lambda c: slice(c * LANES, (c + 1) * LANES)
    s_bufs = (s_a, s_b)
    p_bufs = (p_a, p_b)

    def score(c):
        s_bufs[c % 2][...] = _dot_nt(_split_stack(q_ref[:, lanes(c)], lo), k_ref[...])

    def softmax(c):
        s = s_bufs[c % 2][...]
        p_bufs[c % 2][...] = jnp.exp(s - jnp.max(s, axis=-1, keepdims=True)).astype(BF16)

    def value(c):
        v_aug = jnp.concatenate([v_ref[...], ones], axis=1)
        pv = _dot(p_bufs[c % 2][...], v_aug)
        pv = pv[:, :LANES] / pv[:, LANES:]
        o_ref[:, lanes(c)] = jnp.where(lo, pv[:tq], pv[tq:]).astype(o_ref.dtype)

    for t in range(n_chunks + 2):
        if t < n_chunks:
            score(t)
        if 0 <= t - 1 < n_chunks:
            softmax(t - 1)
        if 0 <= t - 2 < n_chunks:
            value(t - 2)


def _global_attention(qa, ka, va, seq):
    t = qa.shape[0]
    tq = ATTN_Q_TILE
    nq = seq // tq
    return pl.pallas_call(
        functools.partial(_attn_kernel, tq=tq),
        grid=(t // seq, nq),
        in_specs=[pl.BlockSpec((tq, 512), lambda b, j: (b * nq + j, 0)),
                  pl.BlockSpec((seq, LANES), lambda b, j: (b, 0)),
                  pl.BlockSpec((seq, LANES), lambda b, j: (b, 0))],
        out_specs=pl.BlockSpec((tq, 512), lambda b, j: (b * nq + j, 0)),
        out_shape=jax.ShapeDtypeStruct((t, 512), BF16),
        scratch_shapes=[pltpu.VMEM((2 * tq, seq), F32), pltpu.VMEM((2 * tq, seq), F32),
                        pltpu.VMEM((2 * tq, seq), BF16), pltpu.VMEM((2 * tq, seq), BF16)],
        compiler_params=_params(2),
        name="global_attention",
    )(qa, ka, va)


def _pool_kernel(u_ref, w_ref, scale_ref, o_ref, pad_ref, *, seq):
    width = u_ref.shape[1]
    zeros = jnp.zeros((POOL_HALO, width), F32)
    pad_ref[0:POOL_HALO, :] = zeros
    pad_ref[POOL_HALO + seq:, :] = zeros
    pad_ref[POOL_HALO:POOL_HALO + seq, :] = u_ref[...]
    rc = POOL_ROW_CHUNK
    margin = 8
    n_ext = rc + 2 * margin
    assert max(POOL_WINDOWS) // 2 <= margin <= POOL_HALO
    back = lambda y, k: pltpu.roll(y, k, 0)
    ahead = lambda y, k: pltpu.roll(y, n_ext - k, 0)
    for r0 in range(0, seq, rc):
        t = r0 + lax.broadcasted_iota(jnp.int32, (rc, POOL_GROUP), 0)
        for g, w in enumerate(POOL_WINDOWS):
            sl = slice(g * POOL_GROUP, (g + 1) * POOL_GROUP)
            half = w // 2
            base = POOL_HALO + r0 - margin
            run = pad_ref[base:base + n_ext, sl]
            width = 1
            while width < half:
                run = run + ahead(run, width)
                width *= 2
            acc = (back(run, half) + run)[margin:margin + rc]
            cnt = jnp.minimum(t + (w - half), seq) - jnp.maximum(t - half, 0)
            mean = acc / cnt.astype(F32)
            diff = (mean - u_ref[r0:r0 + rc, sl]).astype(BF16)
            y = _dot(diff, w_ref[g]) * scale_ref[:, sl]
            o_ref[r0:r0 + rc, sl] = y.astype(o_ref.dtype)


def _pool_mixer(pu, w_pool, pool_scale, seq):
    t, width = pu.shape
    n_g = len(POOL_WINDOWS)
    return pl.pallas_call(
        functools.partial(_pool_kernel, seq=seq),
        grid=(t // seq,),
        in_specs=[pl.BlockSpec((seq, width), lambda b: (b, 0)),
                  _resident((n_g, POOL_GROUP, POOL_GROUP)),
                  _resident((1, width))],
        out_specs=pl.BlockSpec((seq, width), lambda b: (b, 0)),
        out_shape=jax.ShapeDtypeStruct((t, width), BF16),
        scratch_shapes=[pltpu.VMEM((seq + 2 * POOL_HALO, width), F32)],
        compiler_params=_params(1),
        name="pool_mixer",
    )(pu, w_pool, pool_scale)


def _natten_kernel(q_ref, k_ref, v_ref, bias_ref, o_ref, s_a, s_b, p_a, p_b, *, rows):
    win = NA_ROWS * GRID_W
    n_pairs = C_HEADS // 2
    n_steps = rows // NA_ROW_UNROLL
    lo = lax.broadcasted_iota(jnp.int32, (GRID_W, LANES), 1) < HEAD_DIM
    ones = jnp.ones((win, LANES), BF16)
    lanes = lambda pr: slice(pr * LANES, (pr + 1) * LANES)
    s_bufs = (s_a, s_b)
    p_bufs = (p_a, p_b)

    def jobs_of(step):
        jobs = []
        for u in range(NA_ROW_UNROLL):
            r = step * NA_ROW_UNROLL + u
            rs = min(max(r - NA_ROWS // 2, 0), rows - NA_ROWS)
            kind = rs - r + (NA_ROWS - 1)
            q0 = r * GRID_W
            k0 = rs * GRID_W
            jobs.extend((u * n_pairs + pr, pr, kind, q0, k0) for pr in range(n_pairs))
        return jobs

    def score_job(par, n, pr, kind, q0, k0):
        qs = _split_stack(q_ref[pl.ds(q0, GRID_W), lanes(pr)], lo)
        bias = jnp.concatenate([bias_ref[pr, kind + 2 * jj] for jj in range(NA_ROWS // 2)], axis=1)
        s_bufs[par][n] = _dot_nt(qs, k_ref[pl.ds(k0, win), lanes(pr)]) + bias

    def softmax_job(par, n, pr, kind, q0, k0):
        s = s_bufs[par][n]
        p_bufs[par][n] = jnp.exp(s - jnp.max(s, axis=-1, keepdims=True)).astype(BF16)

    def value_job(par, n, pr, kind, q0, k0):
        v_aug = jnp.concatenate([v_ref[pl.ds(k0, win), lanes(pr)], ones], axis=1)
        pv = _dot(p_bufs[par][n], v_aug)
        pv = pv[:, :LANES] / pv[:, LANES:]
        o_ref[pl.ds(q0, GRID_W), lanes(pr)] = jnp.where(
            lo, pv[:GRID_W], pv[GRID_W:]).astype(o_ref.dtype)

    def emit(score_step=None, softmax_step=None, value_step=None):
        plan = []
        for step, par, fn in (score_step, softmax_step, value_step):
            plan.append([(fn, par, job) for job in jobs_of(step)] if step is not None else [])
        for n in range(NA_ROW_UNROLL * n_pairs):
            for stage in plan:
                if stage:
                    fn, par, job = stage[n]
                    fn(par, *job)

    sc = lambda step, par: (step, par, score_job)
    sm = lambda step, par: (step, par, softmax_job)
    va = lambda step, par: (step, par, value_job)
    none = (None, None, None)

    emit(sc(0, 0), none, none)
    emit(sc(1, 1), sm(0, 0), none)

    def body(m, carry):
        i = 2 * m + 1
        emit(sc(i + 1, 0), sm(i, 1), va(i - 1, 0))
        emit(sc(i + 2, 1), sm(i + 1, 0), va(i, 1))
        return carry

    for m in range((n_steps - 2) // 2):
        body(m, 0)
    emit(none, sm(n_steps - 1, 1), va(n_steps - 2, 0))
    emit(none, none, va(n_steps - 1, 1))


def _neighbourhood_attention(qc, kc, vc, bias, seq):
    t, width = qc.shape
    rows = seq // GRID_W
    n_steps = rows // NA_ROW_UNROLL
    assert rows % NA_ROW_UNROLL == 0 and n_steps % 2 == 0 and n_steps >= 4
    n_jobs = NA_ROW_UNROLL * C_HEADS // 2
    blk = pl.BlockSpec((seq, width), lambda b: (b, 0))
    s_buf = pltpu.VMEM((n_jobs, 2 * GRID_W, NA_ROWS * GRID_W), F32)
    p_buf = pltpu.VMEM((n_jobs, 2 * GRID_W, NA_ROWS * GRID_W), BF16)
    return pl.pallas_call(
        functools.partial(_natten_kernel, rows=rows),
        grid=(t // seq,),
        in_specs=[blk, blk, blk, _resident(bias.shape)],
        out_specs=blk,
        out_shape=jax.ShapeDtypeStruct((t, width), BF16),
        scratch_shapes=[s_buf, s_buf, p_buf, p_buf],
        compiler_params=_params(1),
        name="neighbourhood_attention",
    )(qc, kc, vc, bias)


def _merge_kernel(x_ref, ya_ref, yb_ref, yc_ref, gpre_ref, wg_ref, bg_ref, wb_ref, wo_ref,
                  gpost_ref, o_ref):
    x = x_ref[...]
    h = (_rms(x) * gpre_ref[...]).astype(BF16)
    merged = None
    for n, y_ref in enumerate((ya_ref, yb_ref, yc_ref)):
        sl = slice(n * D_MODEL, (n + 1) * D_MODEL)
        gate = jax.nn.sigmoid(_dot(h, wg_ref[:, sl]) + bg_ref[:, sl])
        term = gate * _dot(y_ref[...], wb_ref[n])
        merged = term if merged is None else merged + term
    mix = _dot(merged.astype(BF16), wo_ref[...])
    o_ref[...] = x + _rms(mix) * gpost_ref[...]


def _merge(x2, ya, yb, yc, g_pre, w_gate, b_gate, w_branch, w_out, g_post):
    t = x2.shape[0]
    tm = TOKEN_TILE
    row = lambda width: pl.BlockSpec((tm, width), lambda i: (i, 0))
    return pl.pallas_call(
        _merge_kernel,
        grid=(t // tm,),
        in_specs=[row(D_MODEL), row(BRANCH_WIDTH), row(BRANCH_WIDTH), row(BRANCH_WIDTH),
                  _resident((1, D_MODEL)), _resident((D_MODEL, N_BRANCH * D_MODEL)),
                  _resident((1, N_BRANCH * D_MODEL)),
                  _resident((N_BRANCH, BRANCH_WIDTH, D_MODEL)),
                  _resident((D_MODEL, D_MODEL)), _resident((1, D_MODEL))],
        out_specs=row(D_MODEL),
        out_shape=jax.ShapeDtypeStruct((t, D_MODEL), F32),
        compiler_params=_params(1),
        name="merge",
    )(x2, ya, yb, yc, g_pre, w_gate, b_gate, w_branch, w_out, g_post)


def _ffn_kernel(x_ref, xp_ref, xn_ref, gpre_ref, wup_ref, cw_ref, cb_ref, wdn_ref, gpost_ref,
                o_ref, h_ref, up_a, up_b, act_a, act_b, acc_ref, *, tm, tiles_per_seq, n_chunks):
    i = pl.program_id(0)
    pos = i % tiles_per_seq
    halo = FF_HALO
    has_prev = (pos > 0).astype(F32)
    has_next = (pos < tiles_per_seq - 1).astype(F32)
    xs = jnp.concatenate([xp_ref[...] * has_prev, x_ref[...], xn_ref[...] * has_next], axis=0)
    h_ref[...] = (_rms(xs) * gpre_ref[...]).astype(BF16)
    ups = (up_a, up_b)
    acts = (act_a, act_b)

    up_rows = (tm + 2 * halo) // FF_UP_JOBS
    act_rows = tm // FF_ACT_ROW_JOBS
    down_rows = tm // FF_DOWN_ROW_JOBS

    def hidden(half, j, start, width):
        off = half * D_FF + j * FF_CHUNK + start
        return slice(off, off + width)

    def up_job(j, par, half, rb):
        rows = slice(rb * up_rows, (rb + 1) * up_rows)
        ups[par][half, rows] = _dot(h_ref[rows], wup_ref[:, hidden(half, j, 0, FF_CHUNK)])

    def conv(up_ref, half, j, r0, sl):
        top = halo + r0 - 8
        n_ext = act_rows + 16
        ext = up_ref[half, top:top + n_ext, sl]
        inner = lambda y: y[8:8 + act_rows]
        before = inner(pltpu.roll(ext, 1, 0))
        after = inner(pltpu.roll(ext, n_ext - 1, 0))
        cols = hidden(half, j, sl.start, LANES)
        w = lambda k: cw_ref[k:k + 1, cols]
        return before * w(0) + inner(ext) * w(1) + after * w(2) + cb_ref[:, cols]

    def gelu_tanh(g):
        k1 = -2.0 * GELU_C * LOG2_E
        return g / (1.0 + jnp.exp2(g * (k1 + (k1 * GELU_A) * (g * g))))

    def act_job(j, par, rb, lane_half):
        r0 = rb * act_rows
        sl = slice(lane_half * LANES, (lane_half + 1) * LANES)
        val = conv(ups[par], 0, j, r0, sl)
        gate = conv(ups[par], 1, j, r0, sl)
        acts[par][r0:r0 + act_rows, sl] = (gelu_tanh(gate) * val).astype(BF16)

    def down_job(j, par, rb, blk, first):
        rows = slice(rb * down_rows, (rb + 1) * down_rows)
        cols = slice(blk * FF_CHUNK, (blk + 1) * FF_CHUNK)
        part = _dot(acts[par][rows], wdn_ref[j * FF_CHUNK:(j + 1) * FF_CHUNK, cols])
        if first:
            acc_ref[rows, cols] = part
        else:
            acc_ref[rows, cols] += part

    def step(up=None, act=None, down=None, first=False):
        u = [functools.partial(up_job, *up, half, rb)
             for half in range(2) for rb in range(FF_UP_JOBS)] if up else []
        a = [functools.partial(act_job, *act, rb, lh)
             for rb in range(FF_ACT_ROW_JOBS) for lh in range(FF_CHUNK // LANES)] if act else []
        d = [functools.partial(down_job, *down, rb, blk, first)
             for rb in range(FF_DOWN_ROW_JOBS) for blk in range(D_MODEL // FF_CHUNK)] if down else []
        n_col = D_MODEL // FF_CHUNK
        n_lh = FF_CHUNK // LANES
        for rb in range(FF_ACT_ROW_JOBS):
            for job in u[rb:rb + 1] + a[rb * n_lh:(rb + 1) * n_lh] + d[rb * n_col:(rb + 1) * n_col]:
                job()

    step(up=(0, 0))
    step(up=(1, 1), act=(0, 0), down=(0, 0), first=True)

    def pair(m, carry):
        j = 2 * m + 1
        step(up=(j + 1, 0), act=(j, 1), down=(j, 1))
        step(up=(j + 2, 1), act=(j + 1, 0), down=(j + 1, 0))
        return carry

    for m in range((n_chunks - 3) // 2):
        pair(m, 0)
    step(up=(n_chunks - 1, 0), act=(n_chunks - 2, 1), down=(n_chunks - 2, 1))
    step(act=(n_chunks - 1, 0), down=(n_chunks - 1, 0))
    o_ref[...] = x_ref[...] + _rms(acc_ref[...]) * gpost_ref[...]


def _ffn(x2, g_pre, w_up, conv_w, conv_b, w_down, g_post, seq):
    t = x2.shape[0]
    tm = FF_TOKEN_TILE
    halo = FF_HALO
    tiles_per_seq = seq // tm
    n_chunks = D_FF // FF_CHUNK
    sub = tm // halo
    last = t // halo - 1
    return pl.pallas_call(
        functools.partial(_ffn_kernel, tm=tm, tiles_per_seq=tiles_per_seq, n_chunks=n_chunks),
        grid=(t // tm,),
        in_specs=[pl.BlockSpec((tm, D_MODEL), lambda i: (i, 0)),
                  pl.BlockSpec((halo, D_MODEL), lambda i: (jnp.maximum(i * sub - 1, 0), 0)),
                  pl.BlockSpec((halo, D_MODEL), lambda i: (jnp.minimum((i + 1) * sub, last), 0)),
                  _resident((1, D_MODEL)),
                  _resident((D_MODEL, 2 * D_FF)),
                  _resident((3, 2 * D_FF)),
                  _resident((1, 2 * D_FF)),
                  _resident((D_FF, D_MODEL)),
                  _resident((1, D_MODEL))],
        out_specs=pl.BlockSpec((tm, D_MODEL), lambda i: (i, 0)),
        out_shape=jax.ShapeDtypeStruct((t, D_MODEL), F32),
        scratch_shapes=[pltpu.VMEM((tm + 2 * halo, D_MODEL), BF16),
                        pltpu.VMEM((2, tm + 2 * halo, FF_CHUNK), F32),
                        pltpu.VMEM((2, tm + 2 * halo, FF_CHUNK), F32),
                        pltpu.VMEM((tm, FF_CHUNK), BF16),
                        pltpu.VMEM((tm, FF_CHUNK), BF16),
                        pltpu.VMEM((tm, D_MODEL), F32)],
        compiler_params=_params(1),
        name="conv_gated_mlp",
    )(x2, x2, x2, g_pre, w_up, conv_w, conv_b, w_down, g_post)


def _regroup_q_heads(w, axis):
    groups = A_HEADS // A_KV_HEADS
    shape = w.shape
    w = w.reshape(shape[:axis] + (A_KV_HEADS, groups, HEAD_DIM) + shape[axis + 1:])
    return jnp.swapaxes(w, axis, axis + 1).reshape(shape)


def _rope_tables(gain, scale, seq):
    t = jnp.arange(seq)
    half = HEAD_DIM // 2
    freqs = ROPE_THETA ** (-jnp.arange(0, half, 2, dtype=F32) / half)

    def axis_tables(pos):
        ang = pos.astype(F32)[:, None] * freqs[None, :]
        c, s = jnp.cos(ang), jnp.sin(ang)
        return jnp.concatenate([c, c], -1), jnp.concatenate([-s, s], -1)

    cr, sr = axis_tables(t // GRID_W)
    cc, sc = axis_tables(t % GRID_W)
    cos = jnp.concatenate([cr, cc], -1)
    sin = jnp.concatenate([sr, sc], -1)
    j = np.arange(HEAD_DIM)
    partner = np.where((j % half) < half // 2, j + half // 2, j - half // 2)
    g = gain.astype(F32)
    ctab = cos * (g * scale)[None, :]
    stab = sin * (g[partner] * scale)[None, :]
    return jnp.tile(ctab, (1, 2)), jnp.tile(stab, (1, 2))


def _natten_bias(rpb):
    c = np.arange(GRID_W)
    cs = np.clip(c - NA_COLS // 2, 0, GRID_W - NA_COLS)
    kc = np.arange(GRID_W)
    valid = (kc[None, :] >= cs[:, None]) & (kc[None, :] < cs[:, None] + NA_COLS)
    pad = GRID_W - NA_COLS
    padded = jnp.pad(rpb.astype(F32), ((0, 0), (0, 0), (pad, pad)))
    row_t = jnp.stack([padded[:, :, GRID_W - 1 - ci:2 * GRID_W - 1 - ci] for ci in range(GRID_W)], axis=2)
    row_t = jnp.where(valid[None, None], row_t, MASK_VALUE)
    n_e = 2 * NA_ROWS - 2
    tab = jnp.concatenate([row_t[:, :n_e], row_t[:, 1:n_e + 1]], axis=-1)
    tab = tab.reshape(C_HEADS // 2, 2, n_e, GRID_W, 2 * GRID_W).transpose(0, 2, 1, 3, 4)
    return tab.reshape(C_HEADS // 2, n_e, 2 * GRID_W, 2 * GRID_W)


def kernel(x, norm_mix_pre, norm_mix_post, norm_ffn_pre, norm_ffn_post, w_in, b_gate,
           qk_norm_q, qk_norm_k, w_pool, pool_scale, rpb, w_branch, w_out,
           w_up, conv_w, conv_b, w_down):
    b, seq, d = x.shape
    depth = w_in.shape[0]
    assert d == D_MODEL and seq % TOKEN_TILE == 0 and seq % ATTN_Q_TILE == 0
    assert seq % FF_TOKEN_TILE == 0
    assert seq // GRID_W >= NA_ROWS and seq % POOL_ROW_CHUNK == 0 and D_FF % FF_CHUNK == 0
    n_chunks = D_FF // FF_CHUNK
    assert n_chunks % 2 == 1 and n_chunks >= 3
    hh = np.arange(2 * LANES) % LANES // HEAD_DIM
    ones_bd = jnp.asarray((hh[:, None] == np.arange(LANES)[None, :] // HEAD_DIM) / HEAD_DIM, BF16)
    row2 = lambda v: v.astype(F32).reshape(1, -1)

    x2 = x.reshape(b * seq, d)
    for l in range(depth):
        w_l = w_in[l]
        w_qkv = jnp.concatenate([_regroup_q_heads(w_l[:, :512], 1), w_l[:, 512:W_QKV]],
                                axis=1).astype(BF16)
        w_gate = w_l[:, W_QKV:].astype(BF16)
        cq, sq = _rope_tables(qk_norm_q[l], QK_SCALE, seq)
        ck, sk = _rope_tables(qk_norm_k[l], 1.0, seq)
        wb = w_branch[l]
        wb = jnp.concatenate([_regroup_q_heads(wb[0], 0)[None], wb[1:]], axis=0).astype(BF16)
        wup = w_up[l].astype(BF16)
        cw = conv_w[l].astype(F32)
        cb = row2(conv_b[l])
        wdn = w_down[l].astype(BF16)

        g_pre = row2(norm_mix_pre[l])
        qa, ka, va, pu, qc, kc, vc = _inproj(x2, g_pre, w_qkv, cq, sq, ck, sk, ones_bd, seq)
        ya = _global_attention(qa, ka, va, seq)
        yb = _pool_mixer(pu, w_pool[l].astype(BF16), row2(pool_scale[l]), seq)
        yc = _neighbourhood_attention(qc, kc, vc, _natten_bias(rpb[l]), seq)
        x2 = _merge(x2, ya, yb, yc, g_pre, w_gate, row2(b_gate[l]), wb, w_out[l].astype(BF16),
                    row2(norm_mix_post[l]))
        x2 = _ffn(x2, row2(norm_ffn_pre[l]), wup, cw, cb, wdn, row2(norm_ffn_post[l]), seq)
    return x2.reshape(b, seq, d)
```

```python
import functools

import numpy as np
import jax
import jax.numpy as jnp
from jax import lax
from jax.experimental import pallas as pl
from jax.experimental.pallas import tpu as pltpu

F32 = jnp.float32
BF16 = jnp.bfloat16

D_MODEL = 1024
GRID_W = 64
HEAD_DIM = 64
A_HEADS = 8
A_KV_HEADS = 2
ROPE_THETA = 10000.0
POOL_WINDOWS = (2, 4, 8, 16)
POOL_GROUP = 128
C_HEADS = 8
NA_ROWS = 8
NA_COLS = 16
N_BRANCH = 3
BRANCH_WIDTH = 512
D_FF = 2816
EPS = 1e-6
QK_SCALE = HEAD_DIM ** -0.5
GELU_C = float(np.sqrt(2.0 / np.pi))
GELU_A = 0.044715
LOG2_E = float(np.log2(np.e))

LANES = 128
V7X_VMEM_BYTES = 64 * 1024 * 1024
VMEM_LIMIT = V7X_VMEM_BYTES * 7 // 8

TOKEN_TILE = 1024
ATTN_Q_TILE = 512
POOL_ROW_CHUNK = 256
POOL_HALO = 16
FF_TOKEN_TILE = 512
FF_CHUNK = 256
FF_HALO = 16
FF_UP_JOBS = 2
FF_ACT_ROW_JOBS = 4
FF_DOWN_ROW_JOBS = 4
MASK_VALUE = -1e30
NA_ROW_UNROLL = 2

OFF_QA = 0
OFF_KVA = 512
OFF_POOL = 768
OFF_QC = 1280
OFF_KC = 1792
OFF_VC = 2304
W_QKV = 2816


def _params(n_grid):
    return pltpu.CompilerParams(
        dimension_semantics=("arbitrary",) * n_grid, vmem_limit_bytes=VMEM_LIMIT)


def _resident(shape):
    zeros = (0,) * len(shape)
    return pl.BlockSpec(shape, lambda *_: zeros, pipeline_mode=pl.Buffered(1))


def _rms(x):
    return x * lax.rsqrt(jnp.mean(x * x, axis=-1, keepdims=True) + EPS)


def _dot(a, b):
    return jnp.dot(a, b, preferred_element_type=F32)


def _dot_nt(a, b):
    return lax.dot_general(a, b, (((1,), (1,)), ((), ())), preferred_element_type=F32)


def _inproj_kernel(x_ref, g_ref, w_ref, cq_ref, sq_ref, ck_ref, sk_ref, ones_ref,
                   qa_ref, ka_ref, va_ref, pu_ref, qc_ref, kc_ref, vc_ref,
                   *, tm, tiles_per_seq):
    i = pl.program_id(0)
    h = (_rms(x_ref[...]) * g_ref[...]).astype(BF16)
    row0 = pl.multiple_of((i % tiles_per_seq) * tm, tm)
    lane = lax.broadcasted_iota(jnp.int32, (tm, LANES), 1)
    first_half = (lane % 32) < 16

    def norm_rope(u, c_ref, s_ref):
        u2 = u * u
        hi = u2.astype(BF16)
        lo = (u2 - hi.astype(F32)).astype(BF16)
        ms = _dot(jnp.concatenate([hi, lo], axis=1), ones_ref[...])
        partner = jnp.where(first_half, pltpu.roll(u, LANES - 16, 1), pltpu.roll(u, 16, 1))
        c = c_ref[pl.ds(row0, tm), :]
        s = s_ref[pl.ds(row0, tm), :]
        return (u * c + partner * s) * lax.rsqrt(ms + EPS)

    def q_chunk(c):
        sl = slice(c * LANES, (c + 1) * LANES)
        qa_ref[:, sl] = norm_rope(qa[:, sl], cq_ref, sq_ref).astype(BF16)

    qa = _dot(h, w_ref[:, OFF_QA:OFF_QA + 512])
    kv = _dot(h, w_ref[:, OFF_KVA:OFF_KVA + 256])
    q_chunk(0)
    pu_ref[...] = _dot(h, w_ref[:, OFF_POOL:OFF_POOL + 512])
    q_chunk(1)
    qc_ref[...] = (_dot(h, w_ref[:, OFF_QC:OFF_QC + 512]) * QK_SCALE).astype(BF16)
    q_chunk(2)
    kc_ref[...] = _dot(h, w_ref[:, OFF_KC:OFF_KC + 512]).astype(BF16)
    q_chunk(3)
    vc_ref[...] = _dot(h, w_ref[:, OFF_VC:OFF_VC + 512]).astype(BF16)
    ka_ref[...] = norm_rope(kv[:, :LANES], ck_ref, sk_ref).astype(BF16)
    va_ref[...] = kv[:, LANES:].astype(BF16)


def _inproj(x2, g_pre, w_qkv, cq, sq, ck, sk, ones_bd, seq):
    t = x2.shape[0]
    tm = TOKEN_TILE
    tiles_per_seq = seq // tm
    row = lambda width: pl.BlockSpec((tm, width), lambda i: (i, 0))
    out_widths = (512, LANES, LANES, 512, 512, 512, 512)
    out_dtypes = (BF16, BF16, BF16, F32, BF16, BF16, BF16)
    out_specs = [row(w) for w in out_widths]
    out_shape = [jax.ShapeDtypeStruct((t, w), d) for w, d in zip(out_widths, out_dtypes)]
    return pl.pallas_call(
        functools.partial(_inproj_kernel, tm=tm, tiles_per_seq=tiles_per_seq),
        grid=(t // tm,),
        in_specs=[row(D_MODEL), _resident((1, D_MODEL)), _resident((D_MODEL, W_QKV)),
                  _resident((seq, LANES)), _resident((seq, LANES)),
                  _resident((seq, LANES)), _resident((seq, LANES)),
                  _resident((2 * LANES, LANES))],
        out_specs=out_specs,
        out_shape=out_shape,
        compiler_params=_params(1),
        name="inproj",
    )(x2, g_pre, w_qkv, cq, sq, ck, sk, ones_bd)


def _split_stack(x, lo):
    zero = jnp.zeros_like(x)
    return jnp.concatenate([jnp.where(lo, x, zero), jnp.where(lo, zero, x)], axis=0)


def _attn_kernel(q_ref, k_ref, v_ref, o_ref, s_a, s_b, p_a, p_b, *, tq):
    n_chunks = 4
    seq = k_ref.shape[0]
    lo = lax.broadcasted_iota(jnp.int32, (tq, LANES), 1) < HEAD_DIM
    ones = jnp.ones((seq, LANES), BF16)
    lanes = lambda c: slice(c * LANES, (c + 1) * LANES)
    s_bufs = (s_a, s_b)
    p_bufs = (p_a, p_b)

    def score(c):
        s_bufs[c % 2][...] = _dot_nt(_split_stack(q_ref[:, lanes(c)], lo), k_ref[...])

    def softmax(c):
        s = s_bufs[c % 2][...]
        p_bufs[c % 2][...] = jnp.exp(s - jnp.max(s, axis=-1, keepdims=True)).astype(BF16)

    def value(c):
        v_aug = jnp.concatenate([v_ref[...], ones], axis=1)
        pv = _dot(p_bufs[c % 2][...], v_aug)
        pv = pv[:, :LANES] / pv[:, LANES:]
        o_ref[:, lanes(c)] = jnp.where(lo, pv[:tq], pv[tq:]).astype(o_ref.dtype)

    for t in range(n_chunks + 2):
        if t < n_chunks:
            score(t)
        if 0 <= t - 1 < n_chunks:
            softmax(t - 1)
        if 0 <= t - 2 < n_chunks:
            value(t - 2)


def _global_attention(qa, ka, va, seq):
    t = qa.shape[0]
    tq = ATTN_Q_TILE
    nq = seq // tq
    return pl.pallas_call(
        functools.partial(_attn_kernel, tq=tq),
        grid=(t // seq, nq),
        in_specs=[pl.BlockSpec((tq, 512), lambda b, j: (b * nq + j, 0)),
                  pl.BlockSpec((seq, LANES), lambda b, j: (b, 0)),
                  pl.BlockSpec((seq, LANES), lambda b, j: (b, 0))],
        out_specs=pl.BlockSpec((tq, 512), lambda b, j: (b * nq + j, 0)),
        out_shape=jax.ShapeDtypeStruct((t, 512), BF16),
        scratch_shapes=[pltpu.VMEM((2 * tq, seq), F32), pltpu.VMEM((2 * tq, seq), F32),
                        pltpu.VMEM((2 * tq, seq), BF16), pltpu.VMEM((2 * tq, seq), BF16)],
        compiler_params=_params(2),
        name="global_attention",
    )(qa, ka, va)


def _pool_kernel(u_ref, w_ref, scale_ref, o_ref, pad_ref, *, seq):
    width = u_ref.shape[1]
    zeros = jnp.zeros((POOL_HALO, width), F32)
    pad_ref[0:POOL_HALO, :] = zeros
    pad_ref[POOL_HALO + seq:, :] = zeros
    pad_ref[POOL_HALO:POOL_HALO + seq, :] = u_ref[...]
    rc = POOL_ROW_CHUNK
    margin = 8
    n_ext = rc + 2 * margin
    assert max(POOL_WINDOWS) // 2 <= margin <= POOL_HALO
    back = lambda y, k: pltpu.roll(y, k, 0)
    ahead = lambda y, k: pltpu.roll(y, n_ext - k, 0)
    for r0 in range(0, seq, rc):
        t = r0 + lax.broadcasted_iota(jnp.int32, (rc, POOL_GROUP), 0)
        for g, w in enumerate(POOL_WINDOWS):
            sl = slice(g * POOL_GROUP, (g + 1) * POOL_GROUP)
            half = w // 2
            base = POOL_HALO + r0 - margin
            run = pad_ref[base:base + n_ext, sl]
            width = 1
            while width < half:
                run = run + ahead(run, width)
                width *= 2
            acc = (back(run, half) + run)[margin:margin + rc]
            cnt = jnp.minimum(t + (w - half), seq) - jnp.maximum(t - half, 0)
            mean = acc / cnt.astype(F32)
            diff = (mean - u_ref[r0:r0 + rc, sl]).astype(BF16)
            y = _dot(diff, w_ref[g]) * scale_ref[:, sl]
            o_ref[r0:r0 + rc, sl] = y.astype(o_ref.dtype)


def _pool_mixer(pu, w_pool, pool_scale, seq):
    t, width = pu.shape
    n_g = len(POOL_WINDOWS)
    return pl.pallas_call(
        functools.partial(_pool_kernel, seq=seq),
        grid=(t // seq,),
        in_specs=[pl.BlockSpec((seq, width), lambda b: (b, 0)),
                  _resident((n_g, POOL_GROUP, POOL_GROUP)),
                  _resident((1, width))],
        out_specs=pl.BlockSpec((seq, width), lambda b: (b, 0)),
        out_shape=jax.ShapeDtypeStruct((t, width), BF16),
        scratch_shapes=[pltpu.VMEM((seq + 2 * POOL_HALO, width), F32)],
        compiler_params=_params(1),
        name="pool_mixer",
    )(pu, w_pool, pool_scale)


def _natten_kernel(q_ref, k_ref, v_ref, bias_ref, o_ref, s_a, s_b, p_a, p_b, *, rows):
    win = NA_ROWS * GRID_W
    n_pairs = C_HEADS // 2
    n_steps = rows // NA_ROW_UNROLL
    lo = lax.broadcasted_iota(jnp.int32, (GRID_W, LANES), 1) < HEAD_DIM
    ones = jnp.ones((win, LANES), BF16)
    lanes = lambda pr: slice(pr * LANES, (pr + 1) * LANES)
    s_bufs = (s_a, s_b)
    p_bufs = (p_a, p_b)

    def jobs_of(step):
        jobs = []
        for u in range(NA_ROW_UNROLL):
            r = step * NA_ROW_UNROLL + u
            rs = min(max(r - NA_ROWS // 2, 0), rows - NA_ROWS)
            kind = rs - r + (NA_ROWS - 1)
            q0 = r * GRID_W
            k0 = rs * GRID_W
            jobs.extend((u * n_pairs + pr, pr, kind, q0, k0) for pr in range(n_pairs))
        return jobs

    def score_job(par, n, pr, kind, q0, k0):
        qs = _split_stack(q_ref[pl.ds(q0, GRID_W), lanes(pr)], lo)
        bias = jnp.concatenate([bias_ref[pr, kind + 2 * jj] for jj in range(NA_ROWS // 2)], axis=1)
        s_bufs[par][n] = _dot_nt(qs, k_ref[pl.ds(k0, win), lanes(pr)]) + bias

    def softmax_job(par, n, pr, kind, q0, k0):
        s = s_bufs[par][n]
        p_bufs[par][n] = jnp.exp(s - jnp.max(s, axis=-1, keepdims=True)).astype(BF16)

    def value_job(par, n, pr, kind, q0, k0):
        v_aug = jnp.concatenate([v_ref[pl.ds(k0, win), lanes(pr)], ones], axis=1)
        pv = _dot(p_bufs[par][n], v_aug)
        pv = pv[:, :LANES] / pv[:, LANES:]
        o_ref[pl.ds(q0, GRID_W), lanes(pr)] = jnp.where(
            lo, pv[:GRID_W], pv[GRID_W:]).astype(o_ref.dtype)

    def emit(score_step=None, softmax_step=None, value_step=None):
        plan = []
        for step, par, fn in (score_step, softmax_step, value_step):
            plan.append([(fn, par, job) for job in jobs_of(step)] if step is not None else [])
        for n in range(NA_ROW_UNROLL * n_pairs):
            for stage in plan:
                if stage:
                    fn, par, job = stage[n]
                    fn(par, *job)

    sc = lambda step, par: (step, par, score_job)
    sm = lambda step, par: (step, par, softmax_job)
    va = lambda step, par: (step, par, value_job)
    none = (None, None, None)

    emit(sc(0, 0), none, none)
    emit(sc(1, 1), sm(0, 0), none)

    for i in range(1, n_steps - 1, 2):
        emit(sc(i + 1, 0), sm(i, 1), va(i - 1, 0))
        emit(sc(i + 2, 1), sm(i + 1, 0), va(i, 1))
    emit(none, sm(n_steps - 1, 1), va(n_steps - 2, 0))
    emit(none, none, va(n_steps - 1, 1))


def _neighbourhood_attention(qc, kc, vc, bias, seq):
    t, width = qc.shape
    rows = seq // GRID_W
    n_steps = rows // NA_ROW_UNROLL
    assert rows % NA_ROW_UNROLL == 0 and n_steps % 2 == 0 and n_steps >= 4
    n_jobs = NA_ROW_UNROLL * C_HEADS // 2
    blk = pl.BlockSpec((seq, width), lambda b: (b, 0))
    s_buf = pltpu.VMEM((n_jobs, 2 * GRID_W, NA_ROWS * GRID_W), F32)
    p_buf = pltpu.VMEM((n_jobs, 2 * GRID_W, NA_ROWS * GRID_W), BF16)
    return pl.pallas_call(
        functools.partial(_natten_kernel, rows=rows),
        grid=(t // seq,),
        in_specs=[blk, blk, blk, _resident(bias.shape)],
        out_specs=blk,
        out_shape=jax.ShapeDtypeStruct((t, width), BF16),
        scratch_shapes=[s_buf, s_buf, p_buf, p_buf],
        compiler_params=_params(1),
        name="neighbourhood_attention",
    )(qc, kc, vc, bias)


def _merge_kernel(x_ref, ya_ref, yb_ref, yc_ref, gpre_ref, wg_ref, bg_ref, wb_ref, wo_ref,
                  gpost_ref, o_ref):
    x = x_ref[...]
    h = (_rms(x) * gpre_ref[...]).astype(BF16)
    merged = None
    for n, y_ref in enumerate((ya_ref, yb_ref, yc_ref)):
        sl = slice(n * D_MODEL, (n + 1) * D_MODEL)
        gate = jax.nn.sigmoid(_dot(h, wg_ref[:, sl]) + bg_ref[:, sl])
        term = gate * _dot(y_ref[...], wb_ref[n])
        merged = term if merged is None else merged + term
    mix = _dot(merged.astype(BF16), wo_ref[...])
    o_ref[...] = x + _rms(mix) * gpost_ref[...]


def _merge(x2, ya, yb, yc, g_pre, w_gate, b_gate, w_branch, w_out, g_post):
    t = x2.shape[0]
    tm = TOKEN_TILE
    row = lambda width: pl.BlockSpec((tm, width), lambda i: (i, 0))
    return pl.pallas_call(
        _merge_kernel,
        grid=(t // tm,),
        in_specs=[row(D_MODEL), row(BRANCH_WIDTH), row(BRANCH_WIDTH), row(BRANCH_WIDTH),
                  _resident((1, D_MODEL)), _resident((D_MODEL, N_BRANCH * D_MODEL)),
                  _resident((1, N_BRANCH * D_MODEL)),
                  _resident((N_BRANCH, BRANCH_WIDTH, D_MODEL)),
                  _resident((D_MODEL, D_MODEL)), _resident((1, D_MODEL))],
        out_specs=row(D_MODEL),
        out_shape=jax.ShapeDtypeStruct((t, D_MODEL), F32),
        compiler_params=_params(1),
        name="merge",
    )(x2, ya, yb, yc, g_pre, w_gate, b_gate, w_branch, w_out, g_post)


def _ffn_kernel(x_ref, xp_ref, xn_ref, gpre_ref, wup_ref, cw_ref, cb_ref, wdn_ref, gpost_ref,
                o_ref, h_ref, up_a, up_b, act_a, act_b, acc_ref, *, tm, tiles_per_seq, n_chunks):
    i = pl.program_id(0)
    pos = i % tiles_per_seq
    halo = FF_HALO
    has_prev = (pos > 0).astype(F32)
    has_next = (pos < tiles_per_seq - 1).astype(F32)
    xs = jnp.concatenate([xp_ref[...] * has_prev, x_ref[...], xn_ref[...] * has_next], axis=0)
    h_ref[...] = (_rms(xs) * gpre_ref[...]).astype(BF16)
    ups = (up_a, up_b)
    acts = (act_a, act_b)

    up_rows = (tm + 2 * halo) // FF_UP_JOBS
    act_rows = tm // FF_ACT_ROW_JOBS
    down_rows = tm // FF_DOWN_ROW_JOBS

    def hidden(half, j, start, width):
        off = half * D_FF + j * FF_CHUNK + start
        return slice(off, off + width)

    def up_job(j, par, half, rb):
        rows = slice(rb * up_rows, (rb + 1) * up_rows)
        ups[par][half, rows] = _dot(h_ref[rows], wup_ref[:, hidden(half, j, 0, FF_CHUNK)])

    def conv(up_ref, half, j, r0, sl):
        top = halo + r0 - 8
        n_ext = act_rows + 16
        ext = up_ref[half, top:top + n_ext, sl]
        inner = lambda y: y[8:8 + act_rows]
        before = inner(pltpu.roll(ext, 1, 0))
        after = inner(pltpu.roll(ext, n_ext - 1, 0))
        cols = hidden(half, j, sl.start, LANES)
        w = lambda k: cw_ref[k:k + 1, cols]
        return before * w(0) + inner(ext) * w(1) + after * w(2) + cb_ref[:, cols]

    def gelu_tanh(g):
        k1 = -2.0 * GELU_C * LOG2_E
        return g / (1.0 + jnp.exp2(g * (k1 + (k1 * GELU_A) * (g * g))))

    def act_job(j, par, rb, lane_half):
        r0 = rb * act_rows
        sl = slice(lane_half * LANES, (lane_half + 1) * LANES)
        val = conv(ups[par], 0, j, r0, sl)
        gate = conv(ups[par], 1, j, r0, sl)
        acts[par][r0:r0 + act_rows, sl] = (gelu_tanh(gate) * val).astype(BF16)

    def down_job(j, par, rb, blk, first):
        rows = slice(rb * down_rows, (rb + 1) * down_rows)
        cols = slice(blk * FF_CHUNK, (blk + 1) * FF_CHUNK)
        part = _dot(acts[par][rows], wdn_ref[j * FF_CHUNK:(j + 1) * FF_CHUNK, cols])
        if first:
            acc_ref[rows, cols] = part
        else:
            acc_ref[rows, cols] += part

    def step(up=None, act=None, down=None, first=False):
        u = [functools.partial(up_job, *up, half, rb)
             for half in range(2) for rb in range(FF_UP_JOBS)] if up else []
        a = [functools.partial(act_job, *act, rb, lh)
             for rb in range(FF_ACT_ROW_JOBS) for lh in range(FF_CHUNK // LANES)] if act else []
        d = [functools.partial(down_job, *down, rb, blk, first)
             for rb in range(FF_DOWN_ROW_JOBS) for blk in range(D_MODEL // FF_CHUNK)] if down else []
        n_col = D_MODEL // FF_CHUNK
        n_lh = FF_CHUNK // LANES
        for rb in range(FF_ACT_ROW_JOBS):
            for job in u[rb:rb + 1] + a[rb * n_lh:(rb + 1) * n_lh] + d[rb * n_col:(rb + 1) * n_col]:
                job()

    step(up=(0, 0))
    step(up=(1, 1), act=(0, 0), down=(0, 0), first=True)

    for j in range(1, n_chunks - 2, 2):
        step(up=(j + 1, 0), act=(j, 1), down=(j, 1))
        step(up=(j + 2, 1), act=(j + 1, 0), down=(j + 1, 0))
    step(up=(n_chunks - 1, 0), act=(n_chunks - 2, 1), down=(n_chunks - 2, 1))
    step(act=(n_chunks - 1, 0), down=(n_chunks - 1, 0))
    o_ref[...] = x_ref[...] + _rms(acc_ref[...]) * gpost_ref[...]


def _ffn(x2, g_pre, w_up, conv_w, conv_b, w_down, g_post, seq):
    t = x2.shape[0]
    tm = FF_TOKEN_TILE
    halo = FF_HALO
    tiles_per_seq = seq // tm
    n_chunks = D_FF // FF_CHUNK
    sub = tm // halo
    last = t // halo - 1
    return pl.pallas_call(
        functools.partial(_ffn_kernel, tm=tm, tiles_per_seq=tiles_per_seq, n_chunks=n_chunks),
        grid=(t // tm,),
        in_specs=[pl.BlockSpec((tm, D_MODEL), lambda i: (i, 0)),
                  pl.BlockSpec((halo, D_MODEL), lambda i: (jnp.maximum(i * sub - 1, 0), 0)),
                  pl.BlockSpec((halo, D_MODEL), lambda i: (jnp.minimum((i + 1) * sub, last), 0)),
                  _resident((1, D_MODEL)),
                  _resident((D_MODEL, 2 * D_FF)),
                  _resident((3, 2 * D_FF)),
                  _resident((1, 2 * D_FF)),
                  _resident((D_FF, D_MODEL)),
                  _resident((1, D_MODEL))],
        out_specs=pl.BlockSpec((tm, D_MODEL), lambda i: (i, 0)),
        out_shape=jax.ShapeDtypeStruct((t, D_MODEL), F32),
        scratch_shapes=[pltpu.VMEM((tm + 2 * halo, D_MODEL), BF16),
                        pltpu.VMEM((2, tm + 2 * halo, FF_CHUNK), F32),
                        pltpu.VMEM((2, tm + 2 * halo, FF_CHUNK), F32),
                        pltpu.VMEM((tm, FF_CHUNK), BF16),
                        pltpu.VMEM((tm, FF_CHUNK), BF16),
                        pltpu.VMEM((tm, D_MODEL), F32)],
        compiler_params=_params(1),
        name="conv_gated_mlp",
    )(x2, x2, x2, g_pre, w_up, conv_w, conv_b, w_down, g_post)


def _regroup_q_heads(w, axis):
    groups = A_HEADS // A_KV_HEADS
    shape = w.shape
    w = w.reshape(shape[:axis] + (A_KV_HEADS, groups, HEAD_DIM) + shape[axis + 1:])
    return jnp.swapaxes(w, axis, axis + 1).reshape(shape)


def _rope_tables(gain, scale, seq):
    t = jnp.arange(seq)
    half = HEAD_DIM // 2
    freqs = ROPE_THETA ** (-jnp.arange(0, half, 2, dtype=F32) / half)

    def axis_tables(pos):
        ang = pos.astype(F32)[:, None] * freqs[None, :]
        c, s = jnp.cos(ang), jnp.sin(ang)
        return jnp.concatenate([c, c], -1), jnp.concatenate([-s, s], -1)

    cr, sr = axis_tables(t // GRID_W)
    cc, sc = axis_tables(t % GRID_W)
    cos = jnp.concatenate([cr, cc], -1)
    sin = jnp.concatenate([sr, sc], -1)
    j = np.arange(HEAD_DIM)
    partner = np.where((j % half) < half // 2, j + half // 2, j - half // 2)
    g = gain.astype(F32)
    ctab = cos * (g * scale)[None, :]
    stab = sin * (g[partner] * scale)[None, :]
    return jnp.tile(ctab, (1, 2)), jnp.tile(stab, (1, 2))


def _natten_bias(rpb):
    c = np.arange(GRID_W)
    cs = np.clip(c - NA_COLS // 2, 0, GRID_W - NA_COLS)
    kc = np.arange(GRID_W)
    valid = (kc[None, :] >= cs[:, None]) & (kc[None, :] < cs[:, None] + NA_COLS)
    pad = GRID_W - NA_COLS
    padded = jnp.pad(rpb.astype(F32), ((0, 0), (0, 0), (pad, pad)))
    row_t = jnp.stack([padded[:, :, GRID_W - 1 - ci:2 * GRID_W - 1 - ci] for ci in range(GRID_W)], axis=2)
    row_t = jnp.where(valid[None, None], row_t, MASK_VALUE)
    n_e = 2 * NA_ROWS - 2
    tab = jnp.concatenate([row_t[:, :n_e], row_t[:, 1:n_e + 1]], axis=-1)
    tab = tab.reshape(C_HEADS // 2, 2, n_e, GRID_W, 2 * GRID_W).transpose(0, 2, 1, 3, 4)
    return tab.reshape(C_HEADS // 2, n_e, 2 * GRID_W, 2 * GRID_W)


def kernel(x, norm_mix_pre, norm_mix_post, norm_ffn_pre, norm_ffn_post, w_in, b_gate,
           qk_norm_q, qk_norm_k, w_pool, pool_scale, rpb, w_branch, w_out,
           w_up, conv_w, conv_b, w_down):
    b, seq, d = x.shape
    depth = w_in.shape[0]
    assert d == D_MODEL and seq % TOKEN_TILE == 0 and seq % ATTN_Q_TILE == 0
    assert seq % FF_TOKEN_TILE == 0
    assert seq // GRID_W >= NA_ROWS and seq % POOL_ROW_CHUNK == 0 and D_FF % FF_CHUNK == 0
    n_chunks = D_FF // FF_CHUNK
    assert n_chunks % 2 == 1 and n_chunks >= 3
    hh = np.arange(2 * LANES) % LANES // HEAD_DIM
    ones_bd = jnp.asarray((hh[:, None] == np.arange(LANES)[None, :] // HEAD_DIM) / HEAD_DIM, BF16)
    row2 = lambda v: v.astype(F32).reshape(1, -1)

    x2 = x.reshape(b * seq, d)
    for l in range(depth):
        w_l = w_in[l]
        w_qkv = jnp.concatenate([_regroup_q_heads(w_l[:, :512], 1), w_l[:, 512:W_QKV]],
                                axis=1).astype(BF16)
        w_gate = w_l[:, W_QKV:].astype(BF16)
        cq, sq = _rope_tables(qk_norm_q[l], QK_SCALE, seq)
        ck, sk = _rope_tables(qk_norm_k[l], 1.0, seq)
        wb = w_branch[l]
        wb = jnp.concatenate([_regroup_q_heads(wb[0], 0)[None], wb[1:]], axis=0).astype(BF16)
        wup = w_up[l].astype(BF16)
        cw = conv_w[l].astype(F32)
        cb = row2(conv_b[l])
        wdn = w_down[l].astype(BF16)

        g_pre = row2(norm_mix_pre[l])
        qa, ka, va, pu, qc, kc, vc = _inproj(x2, g_pre, w_qkv, cq, sq, ck, sk, ones_bd, seq)
        ya = _global_attention(qa, ka, va, seq)
        yb = _pool_mixer(pu, w_pool[l].astype(BF16), row2(pool_scale[l]), seq)
        yc = _neighbourhood_attention(qc, kc, vc, _natten_bias(rpb[l]), seq)
        x2 = _merge(x2, ya, yb, yc, g_pre, w_gate, row2(b_gate[l]), wb, w_out[l].astype(BF16),
                    row2(norm_mix_post[l]))
        x2 = _ffn(x2, row2(norm_ffn_pre[l]), wup, cw, cb, wdn, row2(norm_ffn_post[l]), seq)
    return x2.reshape(b, seq, d)
```

```python
import functools

import numpy as np
import jax
import jax.numpy as jnp
from jax import lax
from jax.experimental import pallas as pl
from jax.experimental.pallas import tpu as pltpu

F32 = jnp.float32
BF16 = jnp.bfloat16

D_MODEL = 1024
GRID_W = 64
HEAD_DIM = 64
A_HEADS = 8
A_KV_HEADS = 2
ROPE_THETA = 10000.0
POOL_WINDOWS = (2, 4, 8, 16)
POOL_GROUP = 128
C_HEADS = 8
NA_ROWS = 8
NA_COLS = 16
N_BRANCH = 3
BRANCH_WIDTH = 512
D_FF = 2816
EPS = 1e-6
QK_SCALE = HEAD_DIM ** -0.5
GELU_C = float(np.sqrt(2.0 / np.pi))
GELU_A = 0.044715
LOG2_E = float(np.log2(np.e))

LANES = 128
V7X_VMEM_BYTES = 64 * 1024 * 1024
VMEM_LIMIT = V7X_VMEM_BYTES * 7 // 8

TOKEN_TILE = 1024
ATTN_Q_TILE = 512
POOL_ROW_CHUNK = 256
POOL_HALO = 16
FF_TOKEN_TILE = 512
FF_CHUNK = 256
FF_HALO = 16
FF_UP_JOBS = 2
FF_ACT_ROW_JOBS = 4
FF_DOWN_ROW_JOBS = 4
MASK_VALUE = -1e30
NA_ROW_UNROLL = 2

OFF_QA = 0
OFF_KVA = 512
OFF_POOL = 768
OFF_QC = 1280
OFF_KC = 1792
OFF_VC = 2304
W_QKV = 2816


def _params(n_grid):
    return pltpu.CompilerParams(
        dimension_semantics=("arbitrary",) * n_grid, vmem_limit_bytes=VMEM_LIMIT)


def _resident(shape):
    zeros = (0,) * len(shape)
    return pl.BlockSpec(shape, lambda *_: zeros, pipeline_mode=pl.Buffered(1))


def _rms(x):
    return x * lax.rsqrt(jnp.mean(x * x, axis=-1, keepdims=True) + EPS)


def _dot(a, b):
    return jnp.dot(a, b, preferred_element_type=F32)


def _dot_nt(a, b):
    return lax.dot_general(a, b, (((1,), (1,)), ((), ())), preferred_element_type=F32)


def _inproj_kernel(x_ref, g_ref, w_ref, cq_ref, sq_ref, ck_ref, sk_ref, ones_ref,
                   qa_ref, ka_ref, va_ref, pu_ref, qc_ref, kc_ref, vc_ref,
                   *, tm, tiles_per_seq):
    i = pl.program_id(0)
    h = (_rms(x_ref[...]) * g_ref[...]).astype(BF16)
    row0 = pl.multiple_of((i % tiles_per_seq) * tm, tm)
    lane = lax.broadcasted_iota(jnp.int32, (tm, LANES), 1)
    first_half = (lane % 32) < 16

    def norm_rope(u, c_ref, s_ref):
        u2 = u * u
        hi = u2.astype(BF16)
        lo = (u2 - hi.astype(F32)).astype(BF16)
        ms = _dot(jnp.concatenate([hi, lo], axis=1), ones_ref[...])
        partner = jnp.where(first_half, pltpu.roll(u, LANES - 16, 1), pltpu.roll(u, 16, 1))
        c = c_ref[pl.ds(row0, tm), :]
        s = s_ref[pl.ds(row0, tm), :]
        return (u * c + partner * s) * lax.rsqrt(ms + EPS)

    def q_chunk(c):
        sl = slice(c * LANES, (c + 1) * LANES)
        qa_ref[:, sl] = norm_rope(qa[:, sl], cq_ref, sq_ref).astype(BF16)

    qa = _dot(h, w_ref[:, OFF_QA:OFF_QA + 512])
    kv = _dot(h, w_ref[:, OFF_KVA:OFF_KVA + 256])
    q_chunk(0)
    pu_ref[...] = _dot(h, w_ref[:, OFF_POOL:OFF_POOL + 512])
    q_chunk(1)
    qc_ref[...] = (_dot(h, w_ref[:, OFF_QC:OFF_QC + 512]) * QK_SCALE).astype(BF16)
    q_chunk(2)
    kc_ref[...] = _dot(h, w_ref[:, OFF_KC:OFF_KC + 512]).astype(BF16)
    q_chunk(3)
    vc_ref[...] = _dot(h, w_ref[:, OFF_VC:OFF_VC + 512]).astype(BF16)
    ka_ref[...] = norm_rope(kv[:, :LANES], ck_ref, sk_ref).astype(BF16)
    va_ref[...] = kv[:, LANES:].astype(BF16)


def _inproj(x2, g_pre, w_qkv, cq, sq, ck, sk, ones_bd, seq):
    t = x2.shape[0]
    tm = TOKEN_TILE
    tiles_per_seq = seq // tm
    row = lambda width: pl.BlockSpec((tm, width), lambda i: (i, 0))
    out_widths = (512, LANES, LANES, 512, 512, 512, 512)
    out_dtypes = (BF16, BF16, BF16, F32, BF16, BF16, BF16)
    out_specs = [row(w) for w in out_widths]
    out_shape = [jax.ShapeDtypeStruct((t, w), d) for w, d in zip(out_widths, out_dtypes)]
    return pl.pallas_call(
        functools.partial(_inproj_kernel, tm=tm, tiles_per_seq=tiles_per_seq),
        grid=(t // tm,),
        in_specs=[row(D_MODEL), _resident((1, D_MODEL)), _resident((D_MODEL, W_QKV)),
                  _resident((seq, LANES)), _resident((seq, LANES)),
                  _resident((seq, LANES)), _resident((seq, LANES)),
                  _resident((2 * LANES, LANES))],
        out_specs=out_specs,
        out_shape=out_shape,
        compiler_params=_params(1),
        name="inproj",
    )(x2, g_pre, w_qkv, cq, sq, ck, sk, ones_bd)


def _split_stack(x, lo):
    zero = jnp.zeros_like(x)
    return jnp.concatenate([jnp.where(lo, x, zero), jnp.where(lo, zero, x)], axis=0)


def _attn_kernel(q_ref, k_ref, v_ref, o_ref, s_a, s_b, p_a, p_b, *, tq):
    n_chunks = 4
    seq = k_ref.shape[0]
    lo = lax.broadcasted_iota(jnp.int32, (tq, LANES), 1) < HEAD_DIM
    ones = jnp.ones((seq, LANES), BF16)
    lanes = lambda c: slice(c * LANES, (c + 1) * LANES)
    s_bufs = (s_a, s_b)
    p_bufs = (p_a, p_b)

    def score(c):
        s_bufs[c % 2][...] = _dot_nt(_split_stack(q_ref[:, lanes(c)], lo), k_ref[...])

    def softmax(c):
        s = s_bufs[c % 2][...]
        p_bufs[c % 2][...] = jnp.exp(s - jnp.max(s, axis=-1, keepdims=True)).astype(BF16)

    def value(c):
        v_aug = jnp.concatenate([v_ref[...], ones], axis=1)
        pv = _dot(p_bufs[c % 2][...], v_aug)
        pv = pv[:, :LANES] / pv[:, LANES:]
        o_ref[:, lanes(c)] = jnp.where(lo, pv[:tq], pv[tq:]).astype(o_ref.dtype)

    for t in range(n_chunks + 2):
        if t < n_chunks:
            score(t)
        if 0 <= t - 1 < n_chunks:
            softmax(t - 1)
        if 0 <= t - 2 < n_chunks:
            value(t - 2)


def _global_attention(qa, ka, va, seq):
    t = qa.shape[0]
    tq = ATTN_Q_TILE
    nq = seq // tq
    return pl.pallas_call(
        functools.partial(_attn_kernel, tq=tq),
        grid=(t // seq, nq),
        in_specs=[pl.BlockSpec((tq, 512), lambda b, j: (b * nq + j, 0)),
                  pl.BlockSpec((seq, LANES), lambda b, j: (b, 0)),
                  pl.BlockSpec((seq, LANES), lambda b, j: (b, 0))],
        out_specs=pl.BlockSpec((tq, 512), lambda b, j: (b * nq + j, 0)),
        out_shape=jax.ShapeDtypeStruct((t, 512), BF16),
        scratch_shapes=[pltpu.VMEM((2 * tq, seq), F32), pltpu.VMEM((2 * tq, seq), F32),
                        pltpu.VMEM((2 * tq, seq), BF16), pltpu.VMEM((2 * tq, seq), BF16)],
        compiler_params=_params(2),
        name="global_attention",
    )(qa, ka, va)


def _pool_jobs(u_ref, w_ref, scale_ref, o_ref, pad_ref, seq):
    def fill():
        zeros = jnp.zeros((POOL_HALO, u_ref.shape[1]), F32)
        pad_ref[0:POOL_HALO, :] = zeros
        pad_ref[POOL_HALO + seq:, :] = zeros
        pad_ref[POOL_HALO:POOL_HALO + seq, :] = u_ref[...]

    rc = POOL_ROW_CHUNK
    margin = 8
    n_ext = rc + 2 * margin
    assert max(POOL_WINDOWS) // 2 <= margin <= POOL_HALO
    back = lambda y, k: pltpu.roll(y, k, 0)
    ahead = lambda y, k: pltpu.roll(y, n_ext - k, 0)
    def chunk(r0):
        t = r0 + lax.broadcasted_iota(jnp.int32, (rc, POOL_GROUP), 0)
        for g, w in enumerate(POOL_WINDOWS):
            sl = slice(g * POOL_GROUP, (g + 1) * POOL_GROUP)
            half = w // 2
            base = POOL_HALO + r0 - margin
            run = pad_ref[base:base + n_ext, sl]
            width = 1
            while width < half:
                run = run + ahead(run, width)
                width *= 2
            acc = (back(run, half) + run)[margin:margin + rc]
            cnt = jnp.minimum(t + (w - half), seq) - jnp.maximum(t - half, 0)
            mean = acc / cnt.astype(F32)
            diff = (mean - u_ref[r0:r0 + rc, sl]).astype(BF16)
            y = _dot(diff, w_ref[g]) * scale_ref[:, sl]
            o_ref[r0:r0 + rc, sl] = y.astype(o_ref.dtype)

    return [fill] + [functools.partial(chunk, r0) for r0 in range(0, seq, rc)]


def _natten_kernel(q_ref, k_ref, v_ref, bias_ref, u_ref, wp_ref, scale_ref, o_ref, yb_ref,
                   s_a, s_b, p_a, p_b, pad_ref, *, rows, seq):
    win = NA_ROWS * GRID_W
    n_pairs = C_HEADS // 2
    n_steps = rows // NA_ROW_UNROLL
    lo = lax.broadcasted_iota(jnp.int32, (GRID_W, LANES), 1) < HEAD_DIM
    ones = jnp.ones((win, LANES), BF16)
    lanes = lambda pr: slice(pr * LANES, (pr + 1) * LANES)
    s_bufs = (s_a, s_b)
    p_bufs = (p_a, p_b)

    def jobs_of(step):
        jobs = []
        for u in range(NA_ROW_UNROLL):
            r = step * NA_ROW_UNROLL + u
            rs = min(max(r - NA_ROWS // 2, 0), rows - NA_ROWS)
            kind = rs - r + (NA_ROWS - 1)
            q0 = r * GRID_W
            k0 = rs * GRID_W
            jobs.extend((u * n_pairs + pr, pr, kind, q0, k0) for pr in range(n_pairs))
        return jobs

    def score_job(par, n, pr, kind, q0, k0):
        qs = _split_stack(q_ref[pl.ds(q0, GRID_W), lanes(pr)], lo)
        bias = jnp.concatenate([bias_ref[pr, kind + 2 * jj] for jj in range(NA_ROWS // 2)], axis=1)
        s_bufs[par][n] = _dot_nt(qs, k_ref[pl.ds(k0, win), lanes(pr)]) + bias

    def softmax_job(par, n, pr, kind, q0, k0):
        s = s_bufs[par][n]
        p_bufs[par][n] = jnp.exp(s - jnp.max(s, axis=-1, keepdims=True)).astype(BF16)

    def value_job(par, n, pr, kind, q0, k0):
        v_aug = jnp.concatenate([v_ref[pl.ds(k0, win), lanes(pr)], ones], axis=1)
        pv = _dot(p_bufs[par][n], v_aug)
        pv = pv[:, :LANES] / pv[:, LANES:]
        o_ref[pl.ds(q0, GRID_W), lanes(pr)] = jnp.where(
            lo, pv[:GRID_W], pv[GRID_W:]).astype(o_ref.dtype)

    def emit(score_step=None, softmax_step=None, value_step=None):
        plan = []
        for step, par, fn in (score_step, softmax_step, value_step):
            plan.append([(fn, par, job) for job in jobs_of(step)] if step is not None else [])
        for n in range(NA_ROW_UNROLL * n_pairs):
            for stage in plan:
                if stage:
                    fn, par, job = stage[n]
                    fn(par, *job)

    sc = lambda step, par: (step, par, score_job)
    sm = lambda step, par: (step, par, softmax_job)
    va = lambda step, par: (step, par, value_job)
    none = (None, None, None)

    pool = _pool_jobs(u_ref, wp_ref, scale_ref, yb_ref, pad_ref, seq)

    def pool_job():
        if pool:
            pool.pop(0)()

    emit(sc(0, 0), none, none)
    pool_job()
    emit(sc(1, 1), sm(0, 0), none)
    pool_job()
    for i in range(1, n_steps - 1, 2):
        emit(sc(i + 1, 0), sm(i, 1), va(i - 1, 0))
        pool_job()
        emit(sc(i + 2, 1), sm(i + 1, 0), va(i, 1))
        pool_job()
    emit(none, sm(n_steps - 1, 1), va(n_steps - 2, 0))
    emit(none, none, va(n_steps - 1, 1))
    while pool:
        pool_job()


def _neighbourhood_attention_and_pool(qc, kc, vc, bias, pu, w_pool, pool_scale, seq):
    t, width = qc.shape
    assert pu.shape == (t, width)
    rows = seq // GRID_W
    n_steps = rows // NA_ROW_UNROLL
    assert rows % NA_ROW_UNROLL == 0 and n_steps % 2 == 0 and n_steps >= 4
    n_jobs = NA_ROW_UNROLL * C_HEADS // 2
    blk = pl.BlockSpec((seq, width), lambda b: (b, 0))
    s_buf = pltpu.VMEM((n_jobs, 2 * GRID_W, NA_ROWS * GRID_W), F32)
    p_buf = pltpu.VMEM((n_jobs, 2 * GRID_W, NA_ROWS * GRID_W), BF16)
    return pl.pallas_call(
        functools.partial(_natten_kernel, rows=rows, seq=seq),
        grid=(t // seq,),
        in_specs=[blk, blk, blk, _resident(bias.shape), blk, _resident(w_pool.shape),
                  _resident((1, width))],
        out_specs=[blk, blk],
        out_shape=[jax.ShapeDtypeStruct((t, width), BF16)] * 2,
        scratch_shapes=[s_buf, s_buf, p_buf, p_buf, pltpu.VMEM((seq + 2 * POOL_HALO, width), F32)],
        compiler_params=_params(1),
        name="neighbourhood_attention_pool",
    )(qc, kc, vc, bias, pu, w_pool, pool_scale)


def _merge_kernel(x_ref, ya_ref, yb_ref, yc_ref, gpre_ref, wg_ref, bg_ref, wb_ref, wo_ref,
                  gpost_ref, o_ref):
    x = x_ref[...]
    h = (_rms(x) * gpre_ref[...]).astype(BF16)
    merged = None
    for n, y_ref in enumerate((ya_ref, yb_ref, yc_ref)):
        sl = slice(n * D_MODEL, (n + 1) * D_MODEL)
        gate = jax.nn.sigmoid(_dot(h, wg_ref[:, sl]) + bg_ref[:, sl])
        term = gate * _dot(y_ref[...], wb_ref[n])
        merged = term if merged is None else merged + term
    mix = _dot(merged.astype(BF16), wo_ref[...])
    o_ref[...] = x + _rms(mix) * gpost_ref[...]


def _merge(x2, ya, yb, yc, g_pre, w_gate, b_gate, w_branch, w_out, g_post):
    t = x2.shape[0]
    tm = TOKEN_TILE
    row = lambda width: pl.BlockSpec((tm, width), lambda i: (i, 0))
    return pl.pallas_call(
        _merge_kernel,
        grid=(t // tm,),
        in_specs=[row(D_MODEL), row(BRANCH_WIDTH), row(BRANCH_WIDTH), row(BRANCH_WIDTH),
                  _resident((1, D_MODEL)), _resident((D_MODEL, N_BRANCH * D_MODEL)),
                  _resident((1, N_BRANCH * D_MODEL)),
                  _resident((N_BRANCH, BRANCH_WIDTH, D_MODEL)),
                  _resident((D_MODEL, D_MODEL)), _resident((1, D_MODEL))],
        out_specs=row(D_MODEL),
        out_shape=jax.ShapeDtypeStruct((t, D_MODEL), F32),
        compiler_params=_params(1),
        name="merge",
    )(x2, ya, yb, yc, g_pre, w_gate, b_gate, w_branch, w_out, g_post)


def _ffn_kernel(x_ref, xp_ref, xn_ref, gpre_ref, wup_ref, cw_ref, cb_ref, wdn_ref, gpost_ref,
                o_ref, h_ref, up_a, up_b, act_a, act_b, acc_ref, *, tm, tiles_per_seq, n_chunks):
    i = pl.program_id(0)
    pos = i % tiles_per_seq
    halo = FF_HALO
    has_prev = (pos > 0).astype(F32)
    has_next = (pos < tiles_per_seq - 1).astype(F32)
    xs = jnp.concatenate([xp_ref[...] * has_prev, x_ref[...], xn_ref[...] * has_next], axis=0)
    h_ref[...] = (_rms(xs) * gpre_ref[...]).astype(BF16)
    ups = (up_a, up_b)
    acts = (act_a, act_b)

    up_rows = (tm + 2 * halo) // FF_UP_JOBS
    act_rows = tm // FF_ACT_ROW_JOBS
    down_rows = tm // FF_DOWN_ROW_JOBS

    def hidden(half, j, start, width):
        off = half * D_FF + j * FF_CHUNK + start
        return slice(off, off + width)

    def up_job(j, par, half, rb):
        rows = slice(rb * up_rows, (rb + 1) * up_rows)
        ups[par][half, rows] = _dot(h_ref[rows], wup_ref[:, hidden(half, j, 0, FF_CHUNK)])

    def conv(up_ref, half, j, r0, sl):
        top = halo + r0 - 8
        n_ext = act_rows + 16
        ext = up_ref[half, top:top + n_ext, sl]
        inner = lambda y: y[8:8 + act_rows]
        before = inner(pltpu.roll(ext, 1, 0))
        after = inner(pltpu.roll(ext, n_ext - 1, 0))
        cols = hidden(half, j, sl.start, LANES)
        w = lambda k: cw_ref[k:k + 1, cols]
        return before * w(0) + inner(ext) * w(1) + after * w(2) + cb_ref[:, cols]

    def gelu_tanh(g):
        k1 = -2.0 * GELU_C * LOG2_E
        return g / (1.0 + jnp.exp2(g * (k1 + (k1 * GELU_A) * (g * g))))

    def act_job(j, par, rb, lane_half):
        r0 = rb * act_rows
        sl = slice(lane_half * LANES, (lane_half + 1) * LANES)
        val = conv(ups[par], 0, j, r0, sl)
        gate = conv(ups[par], 1, j, r0, sl)
        acts[par][r0:r0 + act_rows, sl] = (gelu_tanh(gate) * val).astype(BF16)

    def down_job(j, par, rb, blk, first):
        rows = slice(rb * down_rows, (rb + 1) * down_rows)
        cols = slice(blk * FF_CHUNK, (blk + 1) * FF_CHUNK)
        part = _dot(acts[par][rows], wdn_ref[j * FF_CHUNK:(j + 1) * FF_CHUNK, cols])
        if first:
            acc_ref[rows, cols] = part
        else:
            acc_ref[rows, cols] += part

    def step(up=None, act=None, down=None, first=False):
        u = [functools.partial(up_job, *up, half, rb)
             for half in range(2) for rb in range(FF_UP_JOBS)] if up else []
        a = [functools.partial(act_job, *act, rb, lh)
             for rb in range(FF_ACT_ROW_JOBS) for lh in range(FF_CHUNK // LANES)] if act else []
        d = [functools.partial(down_job, *down, rb, blk, first)
             for rb in range(FF_DOWN_ROW_JOBS) for blk in range(D_MODEL // FF_CHUNK)] if down else []
        n_col = D_MODEL // FF_CHUNK
        n_lh = FF_CHUNK // LANES
        for rb in range(FF_ACT_ROW_JOBS):
            for job in u[rb:rb + 1] + a[rb * n_lh:(rb + 1) * n_lh] + d[rb * n_col:(rb + 1) * n_col]:
                job()

    step(up=(0, 0))
    step(up=(1, 1), act=(0, 0), down=(0, 0), first=True)

    for j in range(1, n_chunks - 2, 2):
        step(up=(j + 1, 0), act=(j, 1), down=(j, 1))
        step(up=(j + 2, 1), act=(j + 1, 0), down=(j + 1, 0))
    step(up=(n_chunks - 1, 0), act=(n_chunks - 2, 1), down=(n_chunks - 2, 1))
    step(act=(n_chunks - 1, 0), down=(n_chunks - 1, 0))
    o_ref[...] = x_ref[...] + _rms(acc_ref[...]) * gpost_ref[...]


def _ffn(x2, g_pre, w_up, conv_w, conv_b, w_down, g_post, seq):
    t = x2.shape[0]
    tm = FF_TOKEN_TILE
    halo = FF_HALO
    tiles_per_seq = seq // tm
    n_chunks = D_FF // FF_CHUNK
    sub = tm // halo
    last = t // halo - 1
    return pl.pallas_call(
        functools.partial(_ffn_kernel, tm=tm, tiles_per_seq=tiles_per_seq, n_chunks=n_chunks),
        grid=(t // tm,),
        in_specs=[pl.BlockSpec((tm, D_MODEL), lambda i: (i, 0)),
                  pl.BlockSpec((halo, D_MODEL), lambda i: (jnp.maximum(i * sub - 1, 0), 0)),
                  pl.BlockSpec((halo, D_MODEL), lambda i: (jnp.minimum((i + 1) * sub, last), 0)),
                  _resident((1, D_MODEL)),
                  _resident((D_MODEL, 2 * D_FF)),
                  _resident((3, 2 * D_FF)),
                  _resident((1, 2 * D_FF)),
                  _resident((D_FF, D_MODEL)),
                  _resident((1, D_MODEL))],
        out_specs=pl.BlockSpec((tm, D_MODEL), lambda i: (i, 0)),
        out_shape=jax.ShapeDtypeStruct((t, D_MODEL), F32),
        scratch_shapes=[pltpu.VMEM((tm + 2 * halo, D_MODEL), BF16),
                        pltpu.VMEM((2, tm + 2 * halo, FF_CHUNK), F32),
                        pltpu.VMEM((2, tm + 2 * halo, FF_CHUNK), F32),
                        pltpu.VMEM((tm, FF_CHUNK), BF16),
                        pltpu.VMEM((tm, FF_CHUNK), BF16),
                        pltpu.VMEM((tm, D_MODEL), F32)],
        compiler_params=_params(1),
        name="conv_gated_mlp",
    )(x2, x2, x2, g_pre, w_up, conv_w, conv_b, w_down, g_post)


def _regroup_q_heads(w, axis):
    groups = A_HEADS // A_KV_HEADS
    shape = w.shape
    w = w.reshape(shape[:axis] + (A_KV_HEADS, groups, HEAD_DIM) + shape[axis + 1:])
    return jnp.swapaxes(w, axis, axis + 1).reshape(shape)


def _rope_tables(gain, scale, seq):
    t = jnp.arange(seq)
    half = HEAD_DIM // 2
    freqs = ROPE_THETA ** (-jnp.arange(0, half, 2, dtype=F32) / half)

    def axis_tables(pos):
        ang = pos.astype(F32)[:, None] * freqs[None, :]
        c, s = jnp.cos(ang), jnp.sin(ang)
        return jnp.concatenate([c, c], -1), jnp.concatenate([-s, s], -1)

    cr, sr = axis_tables(t // GRID_W)
    cc, sc = axis_tables(t % GRID_W)
    cos = jnp.concatenate([cr, cc], -1)
    sin = jnp.concatenate([sr, sc], -1)
    j = np.arange(HEAD_DIM)
    partner = np.where((j % half) < half // 2, j + half // 2, j - half // 2)
    g = gain.astype(F32)
    ctab = cos * (g * scale)[None, :]
    stab = sin * (g[partner] * scale)[None, :]
    return jnp.tile(ctab, (1, 2)), jnp.tile(stab, (1, 2))


def _natten_bias(rpb):
    c = np.arange(GRID_W)
    cs = np.clip(c - NA_COLS // 2, 0, GRID_W - NA_COLS)
    kc = np.arange(GRID_W)
    valid = (kc[None, :] >= cs[:, None]) & (kc[None, :] < cs[:, None] + NA_COLS)
    pad = GRID_W - NA_COLS
    padded = jnp.pad(rpb.astype(F32), ((0, 0), (0, 0), (pad, pad)))
    row_t = jnp.stack([padded[:, :, GRID_W - 1 - ci:2 * GRID_W - 1 - ci] for ci in range(GRID_W)], axis=2)
    row_t = jnp.where(valid[None, None], row_t, MASK_VALUE)
    n_e = 2 * NA_ROWS - 2
    tab = jnp.concatenate([row_t[:, :n_e], row_t[:, 1:n_e + 1]], axis=-1)
    tab = tab.reshape(C_HEADS // 2, 2, n_e, GRID_W, 2 * GRID_W).transpose(0, 2, 1, 3, 4)
    return tab.reshape(C_HEADS // 2, n_e, 2 * GRID_W, 2 * GRID_W)


def kernel(x, norm_mix_pre, norm_mix_post, norm_ffn_pre, norm_ffn_post, w_in, b_gate,
           qk_norm_q, qk_norm_k, w_pool, pool_scale, rpb, w_branch, w_out,
           w_up, conv_w, conv_b, w_down):
    b, seq, d = x.shape
    depth = w_in.shape[0]
    assert d == D_MODEL and seq % TOKEN_TILE == 0 and seq % ATTN_Q_TILE == 0
    assert seq % FF_TOKEN_TILE == 0
    assert seq // GRID_W >= NA_ROWS and seq % POOL_ROW_CHUNK == 0 and D_FF % FF_CHUNK == 0
    n_chunks = D_FF // FF_CHUNK
    assert n_chunks % 2 == 1 and n_chunks >= 3
    hh = np.arange(2 * LANES) % LANES // HEAD_DIM
    ones_bd = jnp.asarray((hh[:, None] == np.arange(LANES)[None, :] // HEAD_DIM) / HEAD_DIM, BF16)
    row2 = lambda v: v.astype(F32).reshape(1, -1)

    x2 = x.reshape(b * seq, d)
    for l in range(depth):
        w_l = w_in[l]
        w_qkv = jnp.concatenate([_regroup_q_heads(w_l[:, :512], 1), w_l[:, 512:W_QKV]],
                                axis=1).astype(BF16)
        w_gate = w_l[:, W_QKV:].astype(BF16)
        cq, sq = _rope_tables(qk_norm_q[l], QK_SCALE, seq)
        ck, sk = _rope_tables(qk_norm_k[l], 1.0, seq)
        wb = w_branch[l]
        wb = jnp.concatenate([_regroup_q_heads(wb[0], 0)[None], wb[1:]], axis=0).astype(BF16)
        wup = w_up[l].astype(BF16)
        cw = conv_w[l].astype(F32)
        cb = row2(conv_b[l])
        wdn = w_down[l].astype(BF16)

        g_pre = row2(norm_mix_pre[l])
        qa, ka, va, pu, qc, kc, vc = _inproj(x2, g_pre, w_qkv, cq, sq, ck, sk, ones_bd, seq)
        ya = _global_attention(qa, ka, va, seq)
        yc, yb = _neighbourhood_attention_and_pool(qc, kc, vc, _natten_bias(rpb[l]), pu,
                                                   w_pool[l].astype(BF16), row2(pool_scale[l]), seq)
        x2 = _merge(x2, ya, yb, yc, g_pre, w_gate, row2(b_gate[l]), wb, w_out[l].astype(BF16),
                    row2(norm_mix_post[l]))
        x2 = _ffn(x2, row2(norm_ffn_pre[l]), wup, cw, cb, wdn, row2(norm_ffn_post[l]), seq)
    return x2.reshape(b, seq, d)
```
